```python
import math
import jax, jax.numpy as jnp
from jax import lax
import numpy as np

D_MODEL = 1024
BATCH = 16
SEQ = 2048
DEPTH = 1

HEAD_DIM = 64
GROUPS = ((128, 1), (512, 4), (2048, 16))
N_GROUPS = len(GROUPS)
HEADS_PER_GROUP = 8
N_HEADS = N_GROUPS * HEADS_PER_GROUP
ATTN_WIDTH = N_HEADS * HEAD_DIM
ATTN_OUT_WIDTH = HEADS_PER_GROUP * HEAD_DIM
Q_BLOCK = 128
CONV_WIDTH = D_MODEL
CONV_KERNEL = 31
N_BRANCHES = 2
D_FF = -(-8 * D_MODEL // (3 * 256)) * 256
IN_WIDTH = 3 * ATTN_WIDTH + 2 * CONV_WIDTH + N_BRANCHES * D_MODEL
RMS_EPS = 1e-6
LN_EPS = 1e-5

kernel_name = "hybrid_dilated_attn_conformer_conv_gated"


def _alibi_slope_list(n):
    def pow2(m):
        start = 2.0 ** (-8.0 / m)
        return [start ** (i + 1) for i in range(m)]
    if math.log2(n).is_integer():
        return pow2(n)
    c = 2 ** math.floor(math.log2(n))
    return pow2(c) + _alibi_slope_list(2 * c)[0::2][: n - c]


def _alibi_slopes():
    s = sorted(_alibi_slope_list(N_HEADS), reverse=True)
    return np.asarray(s, dtype=np.float32).reshape(N_GROUPS, HEADS_PER_GROUP)


def _rmsnorm(x, g):
    x32 = x.astype(jnp.float32)
    y = x32 * lax.rsqrt(jnp.mean(x32 * x32, axis=-1, keepdims=True) + RMS_EPS)
    return (y * g.astype(jnp.float32)).astype(x.dtype)


def _layernorm(x, g, b):
    x32 = x.astype(jnp.float32)
    mu = jnp.mean(x32, axis=-1, keepdims=True)
    var = jnp.mean(jnp.square(x32 - mu), axis=-1, keepdims=True)
    y = (x32 - mu) * lax.rsqrt(var + LN_EPS)
    return (y * g.astype(jnp.float32) + b.astype(jnp.float32)).astype(x.dtype)


def _dilated_group(q, k, v, slopes, window, dilation):
    B, S, Hg, hd = q.shape
    r = dilation
    L = S // r
    n_back = window // r
    assert n_back <= Q_BLOCK
    nb = -(-L // Q_BLOCK)
    Lp = nb * Q_BLOCK

    def to_sub(t):
        return t.reshape(B, L, r, Hg, hd).transpose(0, 2, 3, 1, 4)

    qb = jnp.pad(to_sub(q), ((0, 0), (0, 0), (0, 0), (0, Lp - L), (0, 0)))
    qb = qb.reshape(B, r, Hg, nb, Q_BLOCK, hd)

    def band(t):
        t = jnp.pad(to_sub(t), ((0, 0), (0, 0), (0, 0), (Q_BLOCK, Lp - L), (0, 0)))
        t = t.reshape(B, r, Hg, nb + 1, Q_BLOCK, hd)
        return jnp.concatenate([t[:, :, :, :-1], t[:, :, :, 1:]], axis=4)

    kb, vb = band(k), band(v)
    qi = jnp.arange(Q_BLOCK)[:, None]
    kj = jnp.arange(2 * Q_BLOCK)[None, :]
    rel = Q_BLOCK + qi - kj
    kpos = (jnp.arange(nb)[:, None, None] - 1) * Q_BLOCK + kj[None]
    valid = (rel >= 0) & (rel <= n_back) & (kpos >= 0)
    dist = (rel * r).astype(jnp.float32)

    s = jnp.einsum('brhnqd,brhnkd->brhnqk', qb, kb).astype(jnp.float32) * (hd ** -0.5)
    s = s - slopes.astype(jnp.float32)[:, None, None, None] * dist
    s = jnp.where(valid, s, -jnp.inf)
    m = jnp.max(s, axis=-1, keepdims=True)
    p = jnp.exp(s - m)
    denom = jnp.sum(p, axis=-1)
    o = jnp.einsum('brhnqk,brhnkd->brhnqd', p, vb.astype(jnp.float32)) / denom[..., None]
    lse = m[..., 0] + jnp.log(denom)

    o = o.reshape(B, r, Hg, Lp, hd)[:, :, :, :L].transpose(0, 3, 1, 2, 4).reshape(B, S, Hg, hd)
    lse = lse.reshape(B, r, Hg, Lp)[:, :, :, :L].transpose(0, 3, 1, 2).reshape(B, S, Hg)
    return o, lse


def _causal_depthwise_conv(u, w, b):
    C = u.shape[-1]
    y = lax.conv_general_dilated(
        u, w[:, None, :], window_strides=(1,), padding=[(CONV_KERNEL - 1, 0)],
        dimension_numbers=('NWC', 'WIO', 'NWC'), feature_group_count=C)
    return y + b


def setup_inputs(seed: int = 0) -> dict:
    key = jax.random.key(seed)
    ks = jax.random.split(key, 17)
    f32 = jnp.float32

    def w(k, shape, fan_in):
        return jax.random.normal(k, shape, f32) * (fan_in ** -0.5)

    def gain(k, shape):
        return 1.0 + 0.05 * jax.random.normal(k, shape, f32)

    D = DEPTH
    return {
        "x": jax.random.normal(ks[0], (BATCH, SEQ, D_MODEL), f32),
        "norm1_g": gain(ks[1], (D, D_MODEL)),
        "w_in": w(ks[2], (D, D_MODEL, IN_WIDTH), D_MODEL),
        "gate_b": 0.1 * jax.random.normal(ks[3], (D, N_BRANCHES * D_MODEL), f32),
        "conv_w": w(ks[4], (D, CONV_KERNEL, CONV_WIDTH), CONV_KERNEL),
        "conv_b": 0.02 * jax.random.normal(ks[5], (D, CONV_WIDTH), f32),
        "conv_ln_g": gain(ks[6], (D, CONV_WIDTH)),
        "conv_ln_b": 0.02 * jax.random.normal(ks[7], (D, CONV_WIDTH), f32),
        "w_conv_out": w(ks[8], (D, CONV_WIDTH, D_MODEL), CONV_WIDTH),
        "w_attn_out": w(ks[9], (D, ATTN_OUT_WIDTH, D_MODEL), ATTN_OUT_WIDTH),
        "w_o": w(ks[10], (D, D_MODEL, D_MODEL), D_MODEL),
        "norm2_g": gain(ks[11], (D, D_MODEL)),
        "w_ffn_gate": w(ks[12], (D, D_MODEL, D_FF), D_MODEL),
        "w_ffn_up": w(ks[13], (D, D_MODEL, D_FF), D_MODEL),
        "w_ffn_down": w(ks[14], (D, D_FF, D_MODEL), D_FF),
        "norm_f_g": gain(ks[15], (D_MODEL,)),
    }


def reference(x, norm1_g, w_in, gate_b, conv_w, conv_b, conv_ln_g, conv_ln_b,
              w_conv_out, w_attn_out, w_o, norm2_g, w_ffn_gate, w_ffn_up,
              w_ffn_down, norm_f_g):
    B, S, _ = x.shape
    slopes = jnp.asarray(_alibi_slopes())
    splits = [ATTN_WIDTH, 2 * ATTN_WIDTH, 3 * ATTN_WIDTH, 3 * ATTN_WIDTH + 2 * CONV_WIDTH]
    for l in range(DEPTH):
        h = _rmsnorm(x, norm1_g[l])
        proj = h @ w_in[l]
        q, k, v, u, g_logits = jnp.split(proj, splits, axis=-1)
        q = q.reshape(B, S, N_GROUPS, HEADS_PER_GROUP, HEAD_DIM)
        k = k.reshape(B, S, N_GROUPS, HEADS_PER_GROUP, HEAD_DIM)
        v = v.reshape(B, S, N_GROUPS, HEADS_PER_GROUP, HEAD_DIM)

        outs, lses = [], []
        for g, (window, dilation) in enumerate(GROUPS):
            o, lse = _dilated_group(q[:, :, g], k[:, :, g], v[:, :, g], slopes[g], window, dilation)
            outs.append(o)
            lses.append(lse)
        alpha = jax.nn.softmax(jnp.stack(lses, axis=0), axis=0)
        y_attn = jnp.sum(alpha[..., None] * jnp.stack(outs, axis=0), axis=0)
        y_attn = y_attn.reshape(B, S, ATTN_OUT_WIDTH).astype(x.dtype) @ w_attn_out[l]

        ua, ub = jnp.split(u, 2, axis=-1)
        c = ua * jax.nn.sigmoid(ub)
        c = _causal_depthwise_conv(c, conv_w[l], conv_b[l])
        c = jax.nn.silu(_layernorm(c, conv_ln_g[l], conv_ln_b[l]))
        y_conv = c @ w_conv_out[l]

        gates = jax.nn.sigmoid(g_logits + gate_b[l])
        g_attn, g_conv = jnp.split(gates, 2, axis=-1)
        x = x + (g_attn * y_attn + g_conv * y_conv) @ w_o[l]

        h2 = _rmsnorm(x, norm2_g[l])
        x = x + (jax.nn.silu(h2 @ w_ffn_gate[l]) * (h2 @ w_ffn_up[l])) @ w_ffn_down[l]
    return _rmsnorm(x, norm_f_g)
```

```python
import functools
import math

import numpy as np
import jax
import jax.numpy as jnp
from jax import lax
from jax.experimental import pallas as pl
from jax.experimental.pallas import tpu as pltpu

D_MODEL = 1024
SEQ = 2048
HEAD_DIM = 64
GROUPS = ((128, 1), (512, 4), (2048, 16))
N_GROUPS = len(GROUPS)
HEADS_PER_GROUP = 8
GROUP_WIDTH = HEADS_PER_GROUP * HEAD_DIM
ATTN_WIDTH = N_GROUPS * GROUP_WIDTH
Q_BLOCK = 128
CONV_WIDTH = D_MODEL
CONV_KERNEL = 31
D_FF = 2816
RMS_EPS = 1e-6
LN_EPS = 1e-5
MASK_BIAS = -1e30

LANES = 128
HEADS_PER_STEP = LANES // HEAD_DIM
TOKEN_TILE = 512
CONV_HALO = 32
CONV_ROWS = 32
NORM_ROWS = 16
FF_CHUNK = 512
VMEM_LIMIT = 56 * 1024 * 1024

U_OFF = 3 * ATTN_WIDTH
GATE_OFF = U_OFF + 2 * CONV_WIDTH

BF16 = jnp.bfloat16
F32 = jnp.float32


def _alibi_slope_table():
    def pow2(m):
        start = 2.0 ** (-8.0 / m)
        return [start ** (i + 1) for i in range(m)]

    def slopes(n):
        if math.log2(n).is_integer():
            return pow2(n)
        c = 2 ** math.floor(math.log2(n))
        return pow2(c) + slopes(2 * c)[0::2][: n - c]

    s = sorted(slopes(N_GROUPS * HEADS_PER_GROUP), reverse=True)
    return np.asarray(s, dtype=np.float32)


def _resident(shape):
    return pl.BlockSpec(shape, lambda *_: (0,) * len(shape), pipeline_mode=pl.Buffered(1))


def _sigmoid(z):
    return 1.0 / (1.0 + jnp.exp(-z))


def _in_proj_kernel(x_ref, g_ref, w_ref, gb_ref,
                    qkv0_ref, qkv1_ref, qkv2_ref, c_ref, gate_ref,
                    hn_ref, h1_ref, h4_ref, h16_ref):
    tm = x_ref.shape[1]
    x = x_ref[0]
    hn = x * lax.rsqrt(jnp.mean(x * x, axis=-1, keepdims=True) + RMS_EPS) * g_ref[...]
    h1_ref[...] = hn.astype(BF16)
    for cb in range(hn_ref.shape[0]):
        hn_ref[cb] = hn[:, cb * LANES:(cb + 1) * LANES]
    for r, h_ref in ((GROUPS[1][1], h4_ref), (GROUPS[2][1], h16_ref)):
        rows = tm // r
        for c in range(r):
            for cb in range(hn_ref.shape[0]):
                h_ref[c * rows:(c + 1) * rows, cb * LANES:(cb + 1) * LANES] = (
                    hn_ref[cb, pl.ds(c, rows, stride=r), :].astype(BF16))

    for g, (h_ref, out_ref) in enumerate(((h1_ref, qkv0_ref), (h4_ref, qkv1_ref), (h16_ref, qkv2_ref))):
        r = GROUPS[g][1]
        rows = tm // r
        for which in range(3):
            col = which * ATTN_WIDTH + g * GROUP_WIDTH
            acc = jnp.dot(h_ref[...], w_ref[:, col:col + GROUP_WIDTH],
                          preferred_element_type=F32).astype(BF16)
            for c in range(r):
                out_ref[which, 0, c] = acc[c * rows:(c + 1) * rows]

    h1 = h1_ref[...]
    for jj in range(CONV_WIDTH // GROUP_WIDTH):
        lo = jj * GROUP_WIDTH
        ua = jnp.dot(h1, w_ref[:, U_OFF + lo:U_OFF + lo + GROUP_WIDTH], preferred_element_type=F32)
        ub = jnp.dot(h1, w_ref[:, U_OFF + CONV_WIDTH + lo:U_OFF + CONV_WIDTH + lo + GROUP_WIDTH],
                     preferred_element_type=F32)
        c_ref[0, :, lo:lo + GROUP_WIDTH] = (ua * _sigmoid(ub)).astype(BF16)
    for jj in range(2 * D_MODEL // GROUP_WIDTH):
        lo = jj * GROUP_WIDTH
        gl = jnp.dot(h1, w_ref[:, GATE_OFF + lo:GATE_OFF + lo + GROUP_WIDTH], preferred_element_type=F32)
        gate_ref[0, :, lo:lo + GROUP_WIDTH] = _sigmoid(gl + gb_ref[:, lo:lo + GROUP_WIDTH]).astype(BF16)


def _in_proj(x, norm_g, w_in, gate_b):
    batch, seq, d = x.shape
    tm = TOKEN_TILE
    tpb = seq // tm
    in_width = w_in.shape[1]
    qkv_shapes, qkv_specs = [], []
    for _, r in GROUPS:
        qkv_shapes.append(jax.ShapeDtypeStruct((3, batch, r, seq // r, GROUP_WIDTH), BF16))
        qkv_specs.append(pl.BlockSpec((3, 1, r, tm // r, GROUP_WIDTH),
                                      lambda i: (0, i // tpb, 0, i % tpb, 0)))
    tile = lambda width: pl.BlockSpec((1, tm, width), lambda i: (i // tpb, i % tpb, 0))
    return pl.pallas_call(
        _in_proj_kernel,
        grid=(batch * tpb,),
        in_specs=[tile(d), _resident((1, d)), _resident((d, in_width)), _resident((1, 2 * d))],
        out_specs=qkv_specs + [tile(CONV_WIDTH), tile(2 * d)],
        out_shape=qkv_shapes + [jax.ShapeDtypeStruct((batch, seq, CONV_WIDTH), BF16),
                                jax.ShapeDtypeStruct((batch, seq, 2 * d), BF16)],
        scratch_shapes=[pltpu.VMEM((d // LANES, tm, LANES), F32), pltpu.VMEM((tm, d), BF16),
                        pltpu.VMEM((tm, d), BF16), pltpu.VMEM((tm, d), BF16)],
        compiler_params=pltpu.CompilerParams(dimension_semantics=("arbitrary",),
                                             vmem_limit_bytes=VMEM_LIMIT),
        name="in_proj",
    )(x, norm_g, w_in, gate_b)


def _attn_kernel(slope_ref, qkv0_ref, qkv1_ref, qkv2_ref, y_ref, bias_ref, acc_ref, m_ref, l_ref):
    pair = pl.program_id(1)
    qb = Q_BLOCK
    lane = lax.broadcasted_iota(jnp.int32, (qb, LANES), 1)
    first_head = lane < HEAD_DIM

    qi = lax.broadcasted_iota(jnp.int32, (qb, 2 * qb), 0)
    kj = lax.broadcasted_iota(jnp.int32, (qb, 2 * qb), 1)
    rel = qb + qi - kj
    valid = (rel >= 0) & (rel <= qb)
    rel_f = rel.astype(F32)
    for g, (_, r) in enumerate(GROUPS):
        for h in range(HEADS_PER_STEP):
            slope = slope_ref[g * HEADS_PER_GROUP + pair * HEADS_PER_STEP + h]
            bias_ref[g * HEADS_PER_STEP + h] = jnp.where(valid, -(slope * r) * rel_f, MASK_BIAS)

    def block(g, qkv_ref, c, n, with_prev):
        r = GROUPS[g][1]
        row = pl.multiple_of(n * qb, qb)
        q = qkv_ref[0, 0, c, pl.ds(row, qb), :]
        if with_prev:
            keys = pl.ds(pl.multiple_of(row - qb, qb), 2 * qb)
        else:
            keys = pl.ds(row, qb)
        k = qkv_ref[1, 0, c, keys, :]
        v = qkv_ref[2, 0, c, keys, :]
        pv, ms, ls = [], [], []
        for h in range(HEADS_PER_STEP):
            head = first_head if h == 0 else jnp.logical_not(first_head)
            qh = jnp.where(head, q, jnp.zeros_like(q)) * (HEAD_DIM ** -0.5)
            s = lax.dot_general(qh, k, (((1,), (1,)), ((), ())), preferred_element_type=F32)
            if with_prev:
                s = s + bias_ref[g * HEADS_PER_STEP + h]
            else:
                s = s + bias_ref[g * HEADS_PER_STEP + h, :, qb:]
            m = jnp.max(s, axis=-1, keepdims=True)
            p = jnp.exp(s - m)
            ls.append(jnp.sum(p, axis=-1, keepdims=True))
            ms.append(m)
            pv.append(jnp.dot(p.astype(BF16), v, preferred_element_type=F32))
        tok = n * (qb * r) + c
        rows = pl.ds(tok, qb, stride=r) if r > 1 else pl.ds(pl.multiple_of(tok, qb), qb)
        acc_ref[g, rows, :] = jnp.where(first_head, pv[0], pv[1])
        m_ref[g, rows, :] = jnp.where(first_head, ms[0], ms[1])
        l_ref[g, rows, :] = jnp.where(first_head, ls[0], ls[1])

    for g, qkv_ref in enumerate((qkv0_ref, qkv1_ref, qkv2_ref)):
        r = GROUPS[g][1]
        nb = SEQ // r // qb
        def first_blocks(c, carry, g=g, qkv_ref=qkv_ref):
            block(g, qkv_ref, c, 0, False)
            return carry
        lax.fori_loop(0, r, first_blocks, 0)
        if nb > 1:
            def later_blocks(i, carry, g=g, qkv_ref=qkv_ref, nb=nb):
                block(g, qkv_ref, i // (nb - 1), i % (nb - 1) + 1, True)
                return carry
            lax.fori_loop(0, r * (nb - 1), later_blocks, 0)

    def mix(i, carry):
        rows = pl.ds(pl.multiple_of(i * (2 * qb), 2 * qb), 2 * qb)
        ms = [m_ref[g, rows, :] for g in range(N_GROUPS)]
        top = jnp.maximum(jnp.maximum(ms[0], ms[1]), ms[2])
        num = jnp.zeros_like(top)
        den = jnp.zeros_like(top)
        for g in range(N_GROUPS):
            w = jnp.exp(ms[g] - top)
            num = num + w * acc_ref[g, rows, :]
            den = den + w * l_ref[g, rows, :]
        y_ref[0, rows, :] = (num / den).astype(BF16)
        return carry
    lax.fori_loop(0, SEQ // (2 * qb), mix, 0)


def _attention(qkvs, slopes):
    batch = qkvs[0].shape[1]
    pairs = GROUP_WIDTH // LANES
    in_specs = [pl.BlockSpec(memory_space=pltpu.SMEM)]
    for _, r in GROUPS:
        in_specs.append(pl.BlockSpec((3, 1, r, SEQ // r, LANES), lambda b, j: (0, b, 0, 0, j)))
    return pl.pallas_call(
        _attn_kernel,
        grid=(batch, pairs),
        in_specs=in_specs,
        out_specs=pl.BlockSpec((1, SEQ, LANES), lambda b, j: (b, 0, j)),
        out_shape=jax.ShapeDtypeStruct((batch, SEQ, GROUP_WIDTH), BF16),
        scratch_shapes=[pltpu.VMEM((N_GROUPS * HEADS_PER_STEP, Q_BLOCK, 2 * Q_BLOCK), F32),
                        pltpu.VMEM((N_GROUPS, SEQ, LANES), F32),
                        pltpu.VMEM((N_GROUPS, SEQ, LANES), F32),
                        pltpu.VMEM((N_GROUPS, SEQ, LANES), F32)],
        compiler_params=pltpu.CompilerParams(dimension_semantics=("arbitrary", "arbitrary"),
                                             vmem_limit_bytes=VMEM_LIMIT),
        name="attn",
    )(slopes, *qkvs)


def _merge_kernel(x_ref, cg_ref, halo_ref, gate_ref, ya_ref,
                  cw_ref, cb_ref, lg_ref, lb_ref, wc_ref, wa_ref, wo_ref,
                  o_ref, cbuf_ref, conv_ref, act_ref):
    tm = x_ref.shape[1]
    first_tile = pl.program_id(1) == 0
    halo = halo_ref[0].astype(F32)
    halo = jnp.where(first_tile, jnp.zeros_like(halo), halo)
    cg = cg_ref[0].astype(F32)
    for cb in range(cbuf_ref.shape[0]):
        cols = slice(cb * LANES, (cb + 1) * LANES)
        cbuf_ref[cb, 0:CONV_HALO, :] = halo[:, cols]
        cbuf_ref[cb, CONV_HALO:, :] = cg[:, cols]

    for cb in range(cbuf_ref.shape[0]):
        cols = slice(cb * LANES, (cb + 1) * LANES)
        taps = [cw_ref[k:k + 1, cols] for k in range(CONV_KERNEL)]
        bias = cb_ref[:, cols]

        def conv_chunk(i, carry, cb=cb, cols=cols, taps=taps, bias=bias):
            row = pl.multiple_of(i * CONV_ROWS, CONV_ROWS)
            acc = jnp.broadcast_to(bias, (CONV_ROWS, LANES))
            for k in range(CONV_KERNEL):
                src = row + (CONV_HALO - CONV_KERNEL + 1 + k)
                acc = acc + cbuf_ref[cb, pl.ds(src, CONV_ROWS), :] * taps[k]
            conv_ref[pl.ds(row, CONV_ROWS), cols] = acc
            return carry
        lax.fori_loop(0, tm // CONV_ROWS, conv_chunk, 0)

    def norm_chunk(i, carry):
        row = pl.multiple_of(i * NORM_ROWS, NORM_ROWS)
        c = conv_ref[pl.ds(row, NORM_ROWS), :]
        mu = jnp.mean(c, axis=-1, keepdims=True)
        d = c - mu
        var = jnp.mean(d * d, axis=-1, keepdims=True)
        y = d * lax.rsqrt(var + LN_EPS) * lg_ref[...] + lb_ref[...]
        act_ref[pl.ds(row, NORM_ROWS), :] = (y * _sigmoid(y)).astype(BF16)
        return carry
    lax.fori_loop(0, tm // NORM_ROWS, norm_chunk, 0)

    y_conv = jnp.dot(act_ref[...], wc_ref[...], preferred_element_type=F32)
    y_attn = jnp.dot(ya_ref[0], wa_ref[...], preferred_element_type=F32)
    g_attn = gate_ref[0, :, :D_MODEL].astype(F32)
    g_conv = gate_ref[0, :, D_MODEL:].astype(F32)
    merged = (g_attn * y_attn + g_conv * y_conv).astype(BF16)
    o_ref[0] = x_ref[0] + jnp.dot(merged, wo_ref[...], preferred_element_type=F32)


def _merge(x, cg, gates, y_attn, conv_w, conv_b, ln_g, ln_b, w_conv_out, w_attn_out, w_o):
    batch, seq, d = x.shape
    tm = TOKEN_TILE
    halo_per_tile = tm // CONV_HALO
    tile = lambda width: pl.BlockSpec((1, tm, width), lambda b, i: (b, i, 0))
    halo = pl.BlockSpec((1, CONV_HALO, CONV_WIDTH),
                        lambda b, i: (b, jnp.maximum(i * halo_per_tile - 1, 0), 0))
    return pl.pallas_call(
        _merge_kernel,
        grid=(batch, seq // tm),
        in_specs=[tile(d), tile(CONV_WIDTH), halo, tile(2 * d), tile(GROUP_WIDTH),
                  _resident((CONV_KERNEL, CONV_WIDTH)), _resident((1, CONV_WIDTH)),
                  _resident((1, CONV_WIDTH)), _resident((1, CONV_WIDTH)),
                  _resident((CONV_WIDTH, d)), _resident((GROUP_WIDTH, d)), _resident((d, d))],
        out_specs=tile(d),
        out_shape=jax.ShapeDtypeStruct((batch, seq, d), F32),
        scratch_shapes=[pltpu.VMEM((CONV_WIDTH // LANES, CONV_HALO + tm, LANES), F32),
                        pltpu.VMEM((tm, CONV_WIDTH), F32),
                        pltpu.VMEM((tm, CONV_WIDTH), BF16)],
        compiler_params=pltpu.CompilerParams(dimension_semantics=("arbitrary", "arbitrary"),
                                             vmem_limit_bytes=VMEM_LIMIT),
        name="merge",
    )(x, cg, cg, gates, y_attn, conv_w, conv_b, ln_g, ln_b, w_conv_out, w_attn_out, w_o)


def _ffn_kernel(x_ref, g2_ref, wg_ref, wu_ref, wd_ref, gf_ref, o_ref, h_ref, *, final_norm):
    x = x_ref[0]
    h = x * lax.rsqrt(jnp.mean(x * x, axis=-1, keepdims=True) + RMS_EPS) * g2_ref[...]
    h_ref[...] = h.astype(BF16)
    out = x
    for lo in range(0, D_FF, FF_CHUNK):
        hi = min(lo + FF_CHUNK, D_FF)
        gate = jnp.dot(h_ref[...], wg_ref[:, lo:hi], preferred_element_type=F32)
        up = jnp.dot(h_ref[...], wu_ref[:, lo:hi], preferred_element_type=F32)
        act = (gate * _sigmoid(gate) * up).astype(BF16)
        out = out + jnp.dot(act, wd_ref[lo:hi, :], preferred_element_type=F32)
    if final_norm:
        out = out * lax.rsqrt(jnp.mean(out * out, axis=-1, keepdims=True) + RMS_EPS) * gf_ref[...]
    o_ref[0] = out


def _ffn(x, norm_g, w_gate, w_up, w_down, norm_f_g, final_norm):
    batch, seq, d = x.shape
    tm = TOKEN_TILE
    tile = pl.BlockSpec((1, tm, d), lambda b, i: (b, i, 0))
    return pl.pallas_call(
        functools.partial(_ffn_kernel, final_norm=final_norm),
        grid=(batch, seq // tm),
        in_specs=[tile, _resident((1, d)), _resident((d, D_FF)), _resident((d, D_FF)),
                  _resident((D_FF, d)), _resident((1, d))],
        out_specs=tile,
        out_shape=jax.ShapeDtypeStruct((batch, seq, d), F32),
        scratch_shapes=[pltpu.VMEM((tm, d), BF16)],
        compiler_params=pltpu.CompilerParams(dimension_semantics=("arbitrary", "arbitrary"),
                                             vmem_limit_bytes=VMEM_LIMIT),
        name="ffn",
    )(x, norm_g, w_gate, w_up, w_down, norm_f_g)


def kernel(x, norm1_g, w_in, gate_b, conv_w, conv_b, conv_ln_g, conv_ln_b, w_conv_out, w_attn_out,
           w_o, norm2_g, w_ffn_gate, w_ffn_up, w_ffn_down, norm_f_g):
    depth = w_in.shape[0]
    slopes = jnp.asarray(_alibi_slope_table())
    row = lambda a: a.reshape(1, -1)
    for l in range(depth):
        *qkvs, cg, gates = _in_proj(x, row(norm1_g[l]), w_in[l].astype(BF16), row(gate_b[l]))
        y_attn = _attention(qkvs, slopes)
        x = _merge(x, cg, gates, y_attn, conv_w[l], row(conv_b[l]), row(conv_ln_g[l]),
                   row(conv_ln_b[l]), w_conv_out[l].astype(BF16), w_attn_out[l].astype(BF16),
                   w_o[l].astype(BF16))
        x = _ffn(x, row(norm2_g[l]), w_ffn_gate[l].astype(BF16), w_ffn_up[l].astype(BF16),
                 w_ffn_down[l].astype(BF16), row(norm_f_g), final_norm=(l == depth - 1))
    return x
```

```python
import functools
import math

import numpy as np
import jax
import jax.numpy as jnp
from jax import lax
from jax.experimental import pallas as pl
from jax.experimental.pallas import tpu as pltpu

D_MODEL = 1024
SEQ = 2048
HEAD_DIM = 64
GROUPS = ((128, 1), (512, 4), (2048, 16))
N_GROUPS = len(GROUPS)
HEADS_PER_GROUP = 8
GROUP_WIDTH = HEADS_PER_GROUP * HEAD_DIM
ATTN_WIDTH = N_GROUPS * GROUP_WIDTH
Q_BLOCK = 128
CONV_WIDTH = D_MODEL
CONV_KERNEL = 31
D_FF = 2816
RMS_EPS = 1e-6
LN_EPS = 1e-5
MASK_BIAS = -1e30

LANES = 128
HEADS_PER_STEP = LANES // HEAD_DIM
TOKEN_TILE = 512
CONV_HALO = 32
CONV_ROWS = 64
NORM_ROWS = 16
NORM_UNROLL = 8
ATTN_UNROLL = 16
FF_CHUNK = 512
VMEM_LIMIT = 56 * 1024 * 1024

U_OFF = 3 * ATTN_WIDTH
GATE_OFF = U_OFF + 2 * CONV_WIDTH

BF16 = jnp.bfloat16
F32 = jnp.float32


def _alibi_slope_table():
    def pow2(m):
        start = 2.0 ** (-8.0 / m)
        return [start ** (i + 1) for i in range(m)]

    def slopes(n):
        if math.log2(n).is_integer():
            return pow2(n)
        c = 2 ** math.floor(math.log2(n))
        return pow2(c) + slopes(2 * c)[0::2][: n - c]

    s = sorted(slopes(N_GROUPS * HEADS_PER_GROUP), reverse=True)
    return np.asarray(s, dtype=np.float32)


def _resident(shape):
    return pl.BlockSpec(shape, lambda *_: (0,) * len(shape), pipeline_mode=pl.Buffered(1))


def _sigmoid(z):
    return 1.0 / (1.0 + jnp.exp(-z))


def _unrolled_loop(count, unroll, fn):
    u = max(d for d in range(1, unroll + 1) if count % d == 0)
    if count == u:
        for i in range(count):
            fn(i)
        return

    def body(it, carry):
        for j in range(u):
            fn(it * u + j)
        return carry
    lax.fori_loop(0, count // u, body, 0)


def _in_proj_kernel(x_ref, g_ref, w_ref, gb_ref,
                    qkv0_ref, qkv1_ref, qkv2_ref, c_ref, gate_ref,
                    hn_ref, h1_ref, h4_ref, h16_ref):
    tm = x_ref.shape[1]
    x = x_ref[0]
    hn = x * lax.rsqrt(jnp.mean(x * x, axis=-1, keepdims=True) + RMS_EPS) * g_ref[...]
    h1_ref[...] = hn.astype(BF16)
    for cb in range(hn_ref.shape[0]):
        hn_ref[cb] = hn[:, cb * LANES:(cb + 1) * LANES]
    for r, h_ref in ((GROUPS[1][1], h4_ref), (GROUPS[2][1], h16_ref)):
        rows = tm // r
        for c in range(r):
            for cb in range(hn_ref.shape[0]):
                h_ref[c * rows:(c + 1) * rows, cb * LANES:(cb + 1) * LANES] = (
                    hn_ref[cb, pl.ds(c, rows, stride=r), :].astype(BF16))

    for g, (h_ref, out_ref) in enumerate(((h1_ref, qkv0_ref), (h4_ref, qkv1_ref), (h16_ref, qkv2_ref))):
        r = GROUPS[g][1]
        rows = tm // r
        for which in range(3):
            col = which * ATTN_WIDTH + g * GROUP_WIDTH
            acc = jnp.dot(h_ref[...], w_ref[:, col:col + GROUP_WIDTH],
                          preferred_element_type=F32).astype(BF16)
            for c in range(r):
                out_ref[which, 0, c] = acc[c * rows:(c + 1) * rows]

    h1 = h1_ref[...]
    for jj in range(CONV_WIDTH // GROUP_WIDTH):
        lo = jj * GROUP_WIDTH
        ua = jnp.dot(h1, w_ref[:, U_OFF + lo:U_OFF + lo + GROUP_WIDTH], preferred_element_type=F32)
        ub = jnp.dot(h1, w_ref[:, U_OFF + CONV_WIDTH + lo:U_OFF + CONV_WIDTH + lo + GROUP_WIDTH],
                     preferred_element_type=F32)
        c_ref[0, :, lo:lo + GROUP_WIDTH] = (ua * _sigmoid(ub)).astype(BF16)
    for jj in range(2 * D_MODEL // GROUP_WIDTH):
        lo = jj * GROUP_WIDTH
        gl = jnp.dot(h1, w_ref[:, GATE_OFF + lo:GATE_OFF + lo + GROUP_WIDTH], preferred_element_type=F32)
        gate_ref[0, :, lo:lo + GROUP_WIDTH] = _sigmoid(gl + gb_ref[:, lo:lo + GROUP_WIDTH]).astype(BF16)


def _in_proj(x, norm_g, w_in, gate_b):
    batch, seq, d = x.shape
    tm = TOKEN_TILE
    tpb = seq // tm
    in_width = w_in.shape[1]
    qkv_shapes, qkv_specs = [], []
    for _, r in GROUPS:
        qkv_shapes.append(jax.ShapeDtypeStruct((3, batch, r, seq // r, GROUP_WIDTH), BF16))
        qkv_specs.append(pl.BlockSpec((3, 1, r, tm // r, GROUP_WIDTH),
                                      lambda i: (0, i // tpb, 0, i % tpb, 0)))
    tile = lambda width: pl.BlockSpec((1, tm, width), lambda i: (i // tpb, i % tpb, 0))
    return pl.pallas_call(
        _in_proj_kernel,
        grid=(batch * tpb,),
        in_specs=[tile(d), _resident((1, d)), _resident((d, in_width)), _resident((1, 2 * d))],
        out_specs=qkv_specs + [tile(CONV_WIDTH), tile(2 * d)],
        out_shape=qkv_shapes + [jax.ShapeDtypeStruct((batch, seq, CONV_WIDTH), BF16),
                                jax.ShapeDtypeStruct((batch, seq, 2 * d), BF16)],
        scratch_shapes=[pltpu.VMEM((d // LANES, tm, LANES), F32), pltpu.VMEM((tm, d), BF16),
                        pltpu.VMEM((tm, d), BF16), pltpu.VMEM((tm, d), BF16)],
        compiler_params=pltpu.CompilerParams(dimension_semantics=("arbitrary",),
                                             vmem_limit_bytes=VMEM_LIMIT),
        name="in_proj",
    )(x, norm_g, w_in, gate_b)


def _attn_kernel(slope_ref, qkv0_ref, qkv1_ref, qkv2_ref, y_ref, bias_ref, acc_ref, m_ref, l_ref):
    pair = pl.program_id(1)
    qb = Q_BLOCK
    lane = lax.broadcasted_iota(jnp.int32, (qb, LANES), 1)
    first_head = lane < HEAD_DIM
    scale = HEAD_DIM ** -0.5
    head_scale = [jnp.where(first_head, scale, 0.0).astype(BF16),
                  jnp.where(first_head, 0.0, scale).astype(BF16)]

    qi = lax.broadcasted_iota(jnp.int32, (qb, 2 * qb), 0)
    kj = lax.broadcasted_iota(jnp.int32, (qb, 2 * qb), 1)
    rel = qb + qi - kj
    valid = (rel >= 0) & (rel <= qb)
    rel_f = rel.astype(F32)
    for g, (_, r) in enumerate(GROUPS):
        for h in range(HEADS_PER_STEP):
            slope = slope_ref[g * HEADS_PER_GROUP + pair * HEADS_PER_STEP + h]
            bias_ref[g, h * qb:(h + 1) * qb, :] = jnp.where(valid, -(slope * r) * rel_f, MASK_BIAS)

    def block(g, qkv_ref, c, n, with_prev):
        r = GROUPS[g][1]
        row = pl.multiple_of(n * qb, qb)
        q = qkv_ref[0, 0, c, pl.ds(row, qb), :]
        if with_prev:
            keys = pl.ds(pl.multiple_of(row - qb, qb), 2 * qb)
            bias = bias_ref[g]
        else:
            keys = pl.ds(row, qb)
            bias = bias_ref[g, :, qb:]
        k = qkv_ref[1, 0, c, keys, :]
        v = qkv_ref[2, 0, c, keys, :]
        q2 = jnp.concatenate([q * head_scale[0], q * head_scale[1]], axis=0)
        s = lax.dot_general(q2, k, (((1,), (1,)), ((), ())), preferred_element_type=F32) + bias
        m = jnp.max(s, axis=-1, keepdims=True)
        p = jnp.exp(s - m).astype(BF16)
        v1 = jnp.concatenate([v, jnp.ones_like(v)], axis=1)
        pv = jnp.dot(p, v1, preferred_element_type=F32)
        tok = n * (qb * r) + c
        rows = pl.ds(tok, qb, stride=r) if r > 1 else pl.ds(pl.multiple_of(tok, qb), qb)
        acc_ref[g, rows, :] = jnp.where(first_head, pv[:qb, :LANES], pv[qb:, :LANES])
        l_ref[g, rows, :] = jnp.where(first_head, pv[:qb, LANES:], pv[qb:, LANES:])
        m_ref[g, rows, :] = jnp.where(first_head, m[:qb], m[qb:])

    for g, qkv_ref in enumerate((qkv0_ref, qkv1_ref, qkv2_ref)):
        r = GROUPS[g][1]
        nb = SEQ // r // qb
        _unrolled_loop(r, ATTN_UNROLL,
                       lambda c, g=g, qkv_ref=qkv_ref: block(g, qkv_ref, c, 0, False))
        if nb > 1:
            _unrolled_loop(r * (nb - 1), ATTN_UNROLL,
                           lambda i, g=g, qkv_ref=qkv_ref, nb=nb:
                           block(g, qkv_ref, i // (nb - 1), i % (nb - 1) + 1, True))

    def mix(i, carry):
        rows = pl.ds(pl.multiple_of(i * (2 * qb), 2 * qb), 2 * qb)
        ms = [m_ref[g, rows, :] for g in range(N_GROUPS)]
        top = jnp.maximum(jnp.maximum(ms[0], ms[1]), ms[2])
        num = jnp.zeros_like(top)
        den = jnp.zeros_like(top)
        for g in range(N_GROUPS):
            w = jnp.exp(ms[g] - top)
            num = num + w * acc_ref[g, rows, :]
            den = den + w * l_ref[g, rows, :]
        y_ref[0, rows, :] = (num / den).astype(BF16)
        return carry
    lax.fori_loop(0, SEQ // (2 * qb), mix, 0)


def _attention(qkvs, slopes):
    batch = qkvs[0].shape[1]
    pairs = GROUP_WIDTH // LANES
    in_specs = [pl.BlockSpec(memory_space=pltpu.SMEM)]
    for _, r in GROUPS:
        in_specs.append(pl.BlockSpec((3, 1, r, SEQ // r, LANES), lambda b, j: (0, b, 0, 0, j)))
    return pl.pallas_call(
        _attn_kernel,
        grid=(batch, pairs),
        in_specs=in_specs,
        out_specs=pl.BlockSpec((1, SEQ, LANES), lambda b, j: (b, 0, j)),
        out_shape=jax.ShapeDtypeStruct((batch, SEQ, GROUP_WIDTH), BF16),
        scratch_shapes=[pltpu.VMEM((N_GROUPS, HEADS_PER_STEP * Q_BLOCK, 2 * Q_BLOCK), F32),
                        pltpu.VMEM((N_GROUPS, SEQ, LANES), F32),
                        pltpu.VMEM((N_GROUPS, SEQ, LANES), F32),
                        pltpu.VMEM((N_GROUPS, SEQ, LANES), F32)],
        compiler_params=pltpu.CompilerParams(dimension_semantics=("arbitrary", "arbitrary"),
                                             vmem_limit_bytes=VMEM_LIMIT),
        name="attn",
    )(slopes, *qkvs)


def _merge_kernel(x_ref, cg_ref, halo_ref, gate_ref, ya_ref,
                  cw_ref, cb_ref, lg_ref, lb_ref, wc_ref, wa_ref, wo_ref,
                  o_ref, cbuf_ref, conv_ref, act_ref):
    tm = x_ref.shape[1]
    first_tile = pl.program_id(1) == 0
    halo = halo_ref[0].astype(F32)
    halo = jnp.where(first_tile, jnp.zeros_like(halo), halo)
    cg = cg_ref[0].astype(F32)
    for cb in range(cbuf_ref.shape[0]):
        cols = slice(cb * LANES, (cb + 1) * LANES)
        cbuf_ref[cb, 0:CONV_HALO, :] = halo[:, cols]
        cbuf_ref[cb, CONV_HALO:, :] = cg[:, cols]

    for cb in range(cbuf_ref.shape[0]):
        cols = slice(cb * LANES, (cb + 1) * LANES)
        taps = [cw_ref[k:k + 1, cols] for k in range(CONV_KERNEL)]
        bias = cb_ref[:, cols]

        def conv_chunk(i, carry, cb=cb, cols=cols, taps=taps, bias=bias):
            row = pl.multiple_of(i * CONV_ROWS, CONV_ROWS)
            acc = jnp.broadcast_to(bias, (CONV_ROWS, LANES))
            for k in range(CONV_KERNEL):
                src = row + (CONV_HALO - CONV_KERNEL + 1 + k)
                acc = acc + cbuf_ref[cb, pl.ds(src, CONV_ROWS), :] * taps[k]
            conv_ref[pl.ds(row, CONV_ROWS), cols] = acc
            return carry
        lax.fori_loop(0, tm // CONV_ROWS, conv_chunk, 0)

    def norm_chunk(i):
        row = pl.multiple_of(i * NORM_ROWS, NORM_ROWS)
        c = conv_ref[pl.ds(row, NORM_ROWS), :]
        mu = jnp.mean(c, axis=-1, keepdims=True)
        d = c - mu
        var = jnp.mean(d * d, axis=-1, keepdims=True)
        y = d * lax.rsqrt(var + LN_EPS) * lg_ref[...] + lb_ref[...]
        act_ref[pl.ds(row, NORM_ROWS), :] = (y * _sigmoid(y)).astype(BF16)
    _unrolled_loop(tm // NORM_ROWS, NORM_UNROLL, norm_chunk)

    y_conv = jnp.dot(act_ref[...], wc_ref[...], preferred_element_type=F32)
    y_attn = jnp.dot(ya_ref[0], wa_ref[...], preferred_element_type=F32)
    g_attn = gate_ref[0, :, :D_MODEL].astype(F32)
    g_conv = gate_ref[0, :, D_MODEL:].astype(F32)
    merged = (g_attn * y_attn + g_conv * y_conv).astype(BF16)
    o_ref[0] = x_ref[0] + jnp.dot(merged, wo_ref[...], preferred_element_type=F32)


def _merge(x, cg, gates, y_attn, conv_w, conv_b, ln_g, ln_b, w_conv_out, w_attn_out, w_o):
    batch, seq, d = x.shape
    tm = TOKEN_TILE
    halo_per_tile = tm // CONV_HALO
    tile = lambda width: pl.BlockSpec((1, tm, width), lambda b, i: (b, i, 0))
    halo = pl.BlockSpec((1, CONV_HALO, CONV_WIDTH),
                        lambda b, i: (b, jnp.maximum(i * halo_per_tile - 1, 0), 0))
    return pl.pallas_call(
        _merge_kernel,
        grid=(batch, seq // tm),
        in_specs=[tile(d), tile(CONV_WIDTH), halo, tile(2 * d), tile(GROUP_WIDTH),
                  _resident((CONV_KERNEL, CONV_WIDTH)), _resident((1, CONV_WIDTH)),
                  _resident((1, CONV_WIDTH)), _resident((1, CONV_WIDTH)),
                  _resident((CONV_WIDTH, d)), _resident((GROUP_WIDTH, d)), _resident((d, d))],
        out_specs=tile(d),
        out_shape=jax.ShapeDtypeStruct((batch, seq, d), F32),
        scratch_shapes=[pltpu.VMEM((CONV_WIDTH // LANES, CONV_HALO + tm, LANES), F32),
                        pltpu.VMEM((tm, CONV_WIDTH), F32),
                        pltpu.VMEM((tm, CONV_WIDTH), BF16)],
        compiler_params=pltpu.CompilerParams(dimension_semantics=("arbitrary", "arbitrary"),
                                             vmem_limit_bytes=VMEM_LIMIT),
        name="merge",
    )(x, cg, cg, gates, y_attn, conv_w, conv_b, ln_g, ln_b, w_conv_out, w_attn_out, w_o)


def _ffn_kernel(x_ref, g2_ref, wg_ref, wu_ref, wd_ref, gf_ref, o_ref, h_ref, *, final_norm):
    x = x_ref[0]
    h = x * lax.rsqrt(jnp.mean(x * x, axis=-1, keepdims=True) + RMS_EPS) * g2_ref[...]
    h_ref[...] = h.astype(BF16)
    out = x
    for lo in range(0, D_FF, FF_CHUNK):
        hi = min(lo + FF_CHUNK, D_FF)
        gate = jnp.dot(h_ref[...], wg_ref[:, lo:hi], preferred_element_type=F32)
        up = jnp.dot(h_ref[...], wu_ref[:, lo:hi], preferred_element_type=F32)
        act = (gate * _sigmoid(gate) * up).astype(BF16)
        out = out + jnp.dot(act, wd_ref[lo:hi, :], preferred_element_type=F32)
    if final_norm:
        out = out * lax.rsqrt(jnp.mean(out * out, axis=-1, keepdims=True) + RMS_EPS) * gf_ref[...]
    o_ref[0] = out


def _ffn(x, norm_g, w_gate, w_up, w_down, norm_f_g, final_norm):
    batch, seq, d = x.shape
    tm = TOKEN_TILE
    tile = pl.BlockSpec((1, tm, d), lambda b, i: (b, i, 0))
    return pl.pallas_call(
        functools.partial(_ffn_kernel, final_norm=final_norm),
        grid=(batch, seq // tm),
        in_specs=[tile, _resident((1, d)), _resident((d, D_FF)), _resident((d, D_FF)),
                  _resident((D_FF, d)), _resident((1, d))],
        out_specs=tile,
        out_shape=jax.ShapeDtypeStruct((batch, seq, d), F32),
        scratch_shapes=[pltpu.VMEM((tm, d), BF16)],
        compiler_params=pltpu.CompilerParams(dimension_semantics=("arbitrary", "arbitrary"),
                                             vmem_limit_bytes=VMEM_LIMIT),
        name="ffn",
    )(x, norm_g, w_gate, w_up, w_down, norm_f_g)


def kernel(x, norm1_g, w_in, gate_b, conv_w, conv_b, conv_ln_g, conv_ln_b, w_conv_out, w_attn_out,
           w_o, norm2_g, w_ffn_gate, w_ffn_up, w_ffn_down, norm_f_g):
    depth = w_in.shape[0]
    slopes = jnp.asarray(_alibi_slope_table())
    row = lambda a: a.reshape(1, -1)
    for l in range(depth):
        *qkvs, cg, gates = _in_proj(x, row(norm1_g[l]), w_in[l].astype(BF16), row(gate_b[l]))
        y_attn = _attention(qkvs, slopes)
        x = _merge(x, cg, gates, y_attn, conv_w[l], row(conv_b[l]), row(conv_ln_g[l]),
                   row(conv_ln_b[l]), w_conv_out[l].astype(BF16), w_attn_out[l].astype(BF16),
                   w_o[l].astype(BF16))
        x = _ffn(x, row(norm2_g[l]), w_ffn_gate[l].astype(BF16), w_ffn_up[l].astype(BF16),
                 w_ffn_down[l].astype(BF16), row(norm_f_g), final_norm=(l == depth - 1))
    return x
```

```python
import functools
import math

import numpy as np
import jax
import jax.numpy as jnp
from jax import lax
from jax.experimental import pallas as pl
from jax.experimental.pallas import tpu as pltpu

D_MODEL = 1024
SEQ = 2048
HEAD_DIM = 64
GROUPS = ((128, 1), (512, 4), (2048, 16))
N_GROUPS = len(GROUPS)
HEADS_PER_GROUP = 8
GROUP_WIDTH = HEADS_PER_GROUP * HEAD_DIM
ATTN_WIDTH = N_GROUPS * GROUP_WIDTH
Q_BLOCK = 128
CONV_WIDTH = D_MODEL
CONV_KERNEL = 31
D_FF = 2816
RMS_EPS = 1e-6
LN_EPS = 1e-5
MASK_BIAS = -1e30

LANES = 128
SUBLANES = 8
HEADS_PER_STEP = LANES // HEAD_DIM
TOKEN_TILE = 512
CONV_HALO = 32
CONV_ROWS = 64
NORM_ROWS = 16
ATTN_UNROLL = 16
FF_CHUNK = 512
VMEM_LIMIT = 56 * 1024 * 1024

U_OFF = 3 * ATTN_WIDTH
GATE_OFF = U_OFF + 2 * CONV_WIDTH

BF16 = jnp.bfloat16
F32 = jnp.float32


def _alibi_slope_table():
    def pow2(m):
        start = 2.0 ** (-8.0 / m)
        return [start ** (i + 1) for i in range(m)]

    def slopes(n):
        if math.log2(n).is_integer():
            return pow2(n)
        c = 2 ** math.floor(math.log2(n))
        return pow2(c) + slopes(2 * c)[0::2][: n - c]

    s = sorted(slopes(N_GROUPS * HEADS_PER_GROUP), reverse=True)
    return np.asarray(s, dtype=np.float32)


def _resident(shape):
    return pl.BlockSpec(shape, lambda *_: (0,) * len(shape), pipeline_mode=pl.Buffered(1))


def _sigmoid(z):
    return 1.0 / (1.0 + jnp.exp(-z))


def _row_pitch(dilation):
    return dilation + 1 if dilation % SUBLANES == 0 else dilation


def _unrolled_loop(count, unroll, fn):
    u = max(d for d in range(1, unroll + 1) if count % d == 0)
    if count == u:
        for i in range(count):
            fn(i)
        return

    def body(it, carry):
        for j in range(u):
            fn(it * u + j)
        return carry
    lax.fori_loop(0, count // u, body, 0)


def _in_proj_kernel(x_ref, g_ref, w_ref, gb_ref, cw_ref, cb_ref, lg_ref, lb_ref,
                    qkv0_ref, qkv1_ref, qkv2_ref, act_ref, gate_ref,
                    hn_ref, h1_ref, h4_ref, h16_ref, cbuf_ref, conv_ref, *, tiles_per_seq):
    tm = x_ref.shape[1]
    n_slabs = cbuf_ref.shape[0]

    seq_start = pl.program_id(0) % tiles_per_seq == 0

    @pl.when(seq_start)
    def _():
        cbuf_ref[:, 0:CONV_HALO, :] = jnp.zeros((n_slabs, CONV_HALO, LANES), F32)

    @pl.when(jnp.logical_not(seq_start))
    def _():
        cbuf_ref[:, 0:CONV_HALO, :] = cbuf_ref[:, tm:tm + CONV_HALO, :]

    x = x_ref[0]
    hn = x * lax.rsqrt(jnp.mean(x * x, axis=-1, keepdims=True) + RMS_EPS) * g_ref[...]
    h1_ref[...] = hn.astype(BF16)
    for cb in range(hn_ref.shape[0]):
        hn_ref[cb] = hn[:, cb * LANES:(cb + 1) * LANES]
    for r, h_ref in ((GROUPS[1][1], h4_ref), (GROUPS[2][1], h16_ref)):
        rows = tm // r
        for c in range(r):
            for cb in range(hn_ref.shape[0]):
                h_ref[c * rows:(c + 1) * rows, cb * LANES:(cb + 1) * LANES] = (
                    hn_ref[cb, pl.ds(c, rows, stride=r), :].astype(BF16))

    h1 = h1_ref[...]
    slabs_per_dot = GROUP_WIDTH // LANES
    for jj in range(CONV_WIDTH // GROUP_WIDTH):
        lo = jj * GROUP_WIDTH
        ua = jnp.dot(h1, w_ref[:, U_OFF + lo:U_OFF + lo + GROUP_WIDTH], preferred_element_type=F32)
        ub = jnp.dot(h1, w_ref[:, U_OFF + CONV_WIDTH + lo:U_OFF + CONV_WIDTH + lo + GROUP_WIDTH],
                     preferred_element_type=F32)
        glu = ua * _sigmoid(ub)
        for s in range(slabs_per_dot):
            cbuf_ref[jj * slabs_per_dot + s, CONV_HALO:, :] = glu[:, s * LANES:(s + 1) * LANES]

    def conv_unit(cb, row):
        cols = slice(cb * LANES, (cb + 1) * LANES)
        acc = jnp.broadcast_to(cb_ref[:, cols], (CONV_ROWS, LANES))
        for k in range(CONV_KERNEL):
            src = row + CONV_HALO - CONV_KERNEL + 1 + k
            acc = acc + cbuf_ref[cb, src:src + CONV_ROWS, :] * cw_ref[k:k + 1, cols]
        conv_ref[row:row + CONV_ROWS, cols] = acc

    def norm_unit(row):
        c = conv_ref[row:row + NORM_ROWS, :]
        mu = jnp.mean(c, axis=-1, keepdims=True)
        d = c - mu
        var = jnp.mean(d * d, axis=-1, keepdims=True)
        y = d * lax.rsqrt(var + LN_EPS) * lg_ref[...] + lb_ref[...]
        act_ref[0, row:row + NORM_ROWS, :] = (y * _sigmoid(y)).astype(BF16)

    def qkv_unit(g, which, h_ref, out_ref):
        r = GROUPS[g][1]
        rows = tm // r
        col = which * ATTN_WIDTH + g * GROUP_WIDTH
        acc = jnp.dot(h_ref[...], w_ref[:, col:col + GROUP_WIDTH],
                      preferred_element_type=F32).astype(BF16)
        for c in range(r):
            out_ref[which, 0, c] = acc[c * rows:(c + 1) * rows]

    def gate_unit(jj):
        lo = jj * GROUP_WIDTH
        gl = jnp.dot(h1, w_ref[:, GATE_OFF + lo:GATE_OFF + lo + GROUP_WIDTH], preferred_element_type=F32)
        gate_ref[0, :, lo:lo + GROUP_WIDTH] = _sigmoid(gl + gb_ref[:, lo:lo + GROUP_WIDTH]).astype(BF16)

    vector_units = []
    for row in range(0, tm, CONV_ROWS):
        vector_units += [functools.partial(conv_unit, cb, row) for cb in range(n_slabs)]
        vector_units += [functools.partial(norm_unit, row + i)
                         for i in range(0, CONV_ROWS, NORM_ROWS)]
    matmul_units = []
    for g, (h_ref, out_ref) in enumerate(((h1_ref, qkv0_ref), (h4_ref, qkv1_ref), (h16_ref, qkv2_ref))):
        matmul_units += [functools.partial(qkv_unit, g, which, h_ref, out_ref) for which in range(3)]
    matmul_units += [functools.partial(gate_unit, jj) for jj in range(2 * D_MODEL // GROUP_WIDTH)]

    done = 0
    for i, matmul in enumerate(matmul_units):
        matmul()
        upto = (i + 1) * len(vector_units) // len(matmul_units)
        for unit in vector_units[done:upto]:
            unit()
        done = upto


def _in_proj(x, norm_g, w_in, gate_b, conv_w, conv_b, ln_g, ln_b):
    batch, seq, d = x.shape
    tm = TOKEN_TILE
    tpb = seq // tm
    in_width = w_in.shape[1]
    qkv_shapes, qkv_specs = [], []
    for _, r in GROUPS:
        qkv_shapes.append(jax.ShapeDtypeStruct((3, batch, r, seq // r, GROUP_WIDTH), BF16))
        qkv_specs.append(pl.BlockSpec((3, 1, r, tm // r, GROUP_WIDTH),
                                      lambda i: (0, i // tpb, 0, i % tpb, 0)))
    tile = lambda width: pl.BlockSpec((1, tm, width), lambda i: (i // tpb, i % tpb, 0))
    return pl.pallas_call(
        functools.partial(_in_proj_kernel, tiles_per_seq=tpb),
        grid=(batch * tpb,),
        in_specs=[tile(d), _resident((1, d)), _resident((d, in_width)), _resident((1, 2 * d)),
                  _resident((CONV_KERNEL, CONV_WIDTH)), _resident((1, CONV_WIDTH)),
                  _resident((1, CONV_WIDTH)), _resident((1, CONV_WIDTH))],
        out_specs=qkv_specs + [tile(CONV_WIDTH), tile(2 * d)],
        out_shape=qkv_shapes + [jax.ShapeDtypeStruct((batch, seq, CONV_WIDTH), BF16),
                                jax.ShapeDtypeStruct((batch, seq, 2 * d), BF16)],
        scratch_shapes=[pltpu.VMEM((d // LANES, tm, LANES), F32), pltpu.VMEM((tm, d), BF16),
                        pltpu.VMEM((tm, d), BF16), pltpu.VMEM((tm, d), BF16),
                        pltpu.VMEM((CONV_WIDTH // LANES, CONV_HALO + tm, LANES), F32),
                        pltpu.VMEM((tm, CONV_WIDTH), F32)],
        compiler_params=pltpu.CompilerParams(dimension_semantics=("arbitrary",),
                                             vmem_limit_bytes=VMEM_LIMIT),
        name="in_proj",
    )(x, norm_g, w_in, gate_b, conv_w, conv_b, ln_g, ln_b)


def _attn_kernel(slope_ref, qkv0_ref, qkv1_ref, qkv2_ref, y_ref, bias_ref, acc_ref, m_ref, l_ref):
    pair = pl.program_id(1)
    qb = Q_BLOCK
    lane = lax.broadcasted_iota(jnp.int32, (qb, LANES), 1)
    first_head = lane < HEAD_DIM
    scale = HEAD_DIM ** -0.5
    head_scale = [jnp.where(first_head, scale, 0.0).astype(BF16),
                  jnp.where(first_head, 0.0, scale).astype(BF16)]

    qi = lax.broadcasted_iota(jnp.int32, (qb, 2 * qb), 0)
    kj = lax.broadcasted_iota(jnp.int32, (qb, 2 * qb), 1)
    rel = qb + qi - kj
    valid = (rel >= 0) & (rel <= qb)
    rel_f = rel.astype(F32)
    for g, (_, r) in enumerate(GROUPS):
        for h in range(HEADS_PER_STEP):
            slope = slope_ref[g * HEADS_PER_GROUP + pair * HEADS_PER_STEP + h]
            bias_ref[g, h * qb:(h + 1) * qb, :] = jnp.where(valid, -(slope * r) * rel_f, MASK_BIAS)

    def block(g, qkv_ref, c, n, with_prev):
        r = GROUPS[g][1]
        row = pl.multiple_of(n * qb, qb)
        q = qkv_ref[0, 0, c, pl.ds(row, qb), :]
        if with_prev:
            keys = pl.ds(pl.multiple_of(row - qb, qb), 2 * qb)
            bias = bias_ref[g]
        else:
            keys = pl.ds(row, qb)
            bias = bias_ref[g, :, qb:]
        k = qkv_ref[1, 0, c, keys, :]
        v = qkv_ref[2, 0, c, keys, :]
        q2 = jnp.concatenate([q * head_scale[0], q * head_scale[1]], axis=0)
        s = lax.dot_general(q2, k, (((1,), (1,)), ((), ())), preferred_element_type=F32) + bias
        m = jnp.max(s, axis=-1, keepdims=True)
        p = jnp.exp(s - m).astype(BF16)
        v1 = jnp.concatenate([v, jnp.ones_like(v)], axis=1)
        pv = jnp.dot(p, v1, preferred_element_type=F32)
        pitch = _row_pitch(r)
        tok = n * (qb * pitch) + c
        rows = pl.ds(tok, qb, stride=pitch) if r > 1 else pl.ds(pl.multiple_of(tok, qb), qb)
        acc_ref[g, rows, :] = jnp.where(first_head, pv[:qb, :LANES], pv[qb:, :LANES])
        l_ref[g, rows, :] = jnp.where(first_head, pv[:qb, LANES:], pv[qb:, LANES:])
        m_ref[g, rows, :] = jnp.where(first_head, m[:qb], m[qb:])

    for g, qkv_ref in enumerate((qkv0_ref, qkv1_ref, qkv2_ref)):
        r = GROUPS[g][1]
        nb = SEQ // r // qb
        _unrolled_loop(r, ATTN_UNROLL,
                       lambda c, g=g, qkv_ref=qkv_ref: block(g, qkv_ref, c, 0, False))
        if nb > 1:
            _unrolled_loop(r * (nb - 1), ATTN_UNROLL,
                           lambda i, g=g, qkv_ref=qkv_ref, nb=nb:
                           block(g, qkv_ref, i // (nb - 1), i % (nb - 1) + 1, True))

    def mix(i, carry):
        chunk = 2 * qb
        rows = pl.ds(pl.multiple_of(i * chunk, chunk), chunk)

        def tokens(ref, g):
            r = GROUPS[g][1]
            pitch = _row_pitch(r)
            if pitch == r:
                return ref[g, rows, :]
            first = i * (chunk // r)
            return jnp.concatenate([ref[g, pl.ds((first + j) * pitch, r), :]
                                    for j in range(chunk // r)], axis=0)

        ms = [tokens(m_ref, g) for g in range(N_GROUPS)]
        top = jnp.maximum(jnp.maximum(ms[0], ms[1]), ms[2])
        num = jnp.zeros_like(top)
        den = jnp.zeros_like(top)
        for g in range(N_GROUPS):
            w = jnp.exp(ms[g] - top)
            num = num + w * tokens(acc_ref, g)
            den = den + w * tokens(l_ref, g)
        y_ref[0, rows, :] = (num / den).astype(BF16)
        return carry
    lax.fori_loop(0, SEQ // (2 * qb), mix, 0)


def _attention(qkvs, slopes):
    batch = qkvs[0].shape[1]
    pairs = GROUP_WIDTH // LANES
    in_specs = [pl.BlockSpec(memory_space=pltpu.SMEM)]
    for _, r in GROUPS:
        in_specs.append(pl.BlockSpec((3, 1, r, SEQ // r, LANES), lambda b, j: (0, b, 0, 0, j)))
    result_rows = max((SEQ // r) * _row_pitch(r) for _, r in GROUPS)
    result_buf = pltpu.VMEM((N_GROUPS, result_rows, LANES), F32)
    return pl.pallas_call(
        _attn_kernel,
        grid=(batch, pairs),
        in_specs=in_specs,
        out_specs=pl.BlockSpec((1, SEQ, LANES), lambda b, j: (b, 0, j)),
        out_shape=jax.ShapeDtypeStruct((batch, SEQ, GROUP_WIDTH), BF16),
        scratch_shapes=[pltpu.VMEM((N_GROUPS, HEADS_PER_STEP * Q_BLOCK, 2 * Q_BLOCK), F32),
                        result_buf, result_buf, result_buf],
        compiler_params=pltpu.CompilerParams(dimension_semantics=("arbitrary", "arbitrary"),
                                             vmem_limit_bytes=VMEM_LIMIT),
        name="attn",
    )(slopes, *qkvs)


def _merge_kernel(x_ref, act_ref, gate_ref, ya_ref, wc_ref, wa_ref, wo_ref, o_ref):
    y_conv = jnp.dot(act_ref[0], wc_ref[...], preferred_element_type=F32)
    y_attn = jnp.dot(ya_ref[0], wa_ref[...], preferred_element_type=F32)
    g_attn = gate_ref[0, :, :D_MODEL].astype(F32)
    g_conv = gate_ref[0, :, D_MODEL:].astype(F32)
    merged = (g_attn * y_attn + g_conv * y_conv).astype(BF16)
    o_ref[0] = x_ref[0] + jnp.dot(merged, wo_ref[...], preferred_element_type=F32)


def _merge(x, act, gates, y_attn, w_conv_out, w_attn_out, w_o):
    batch, seq, d = x.shape
    tm = TOKEN_TILE
    tile = lambda width: pl.BlockSpec((1, tm, width), lambda b, i: (b, i, 0))
    return pl.pallas_call(
        _merge_kernel,
        grid=(batch, seq // tm),
        in_specs=[tile(d), tile(CONV_WIDTH), tile(2 * d), tile(GROUP_WIDTH),
                  _resident((CONV_WIDTH, d)), _resident((GROUP_WIDTH, d)), _resident((d, d))],
        out_specs=tile(d),
        out_shape=jax.ShapeDtypeStruct((batch, seq, d), F32),
        compiler_params=pltpu.CompilerParams(dimension_semantics=("arbitrary", "arbitrary"),
                                             vmem_limit_bytes=VMEM_LIMIT),
        name="merge",
    )(x, act, gates, y_attn, w_conv_out, w_attn_out, w_o)


def _ffn_kernel(x_ref, g2_ref, wg_ref, wu_ref, wd_ref, gf_ref, o_ref, h_ref, *, final_norm):
    x = x_ref[0]
    h = x * lax.rsqrt(jnp.mean(x * x, axis=-1, keepdims=True) + RMS_EPS) * g2_ref[...]
    h_ref[...] = h.astype(BF16)
    out = x
    for lo in range(0, D_FF, FF_CHUNK):
        hi = min(lo + FF_CHUNK, D_FF)
        gate = jnp.dot(h_ref[...], wg_ref[:, lo:hi], preferred_element_type=F32)
        up = jnp.dot(h_ref[...], wu_ref[:, lo:hi], preferred_element_type=F32)
        act = (gate * _sigmoid(gate) * up).astype(BF16)
        out = out + jnp.dot(act, wd_ref[lo:hi, :], preferred_element_type=F32)
    if final_norm:
        out = out * lax.rsqrt(jnp.mean(out * out, axis=-1, keepdims=True) + RMS_EPS) * gf_ref[...]
    o_ref[0] = out


def _ffn(x, norm_g, w_gate, w_up, w_down, norm_f_g, final_norm):
    batch, seq, d = x.shape
    tm = TOKEN_TILE
    tile = pl.BlockSpec((1, tm, d), lambda b, i: (b, i, 0))
    return pl.pallas_call(
        functools.partial(_ffn_kernel, final_norm=final_norm),
        grid=(batch, seq // tm),
        in_specs=[tile, _resident((1, d)), _resident((d, D_FF)), _resident((d, D_FF)),
                  _resident((D_FF, d)), _resident((1, d))],
        out_specs=tile,
        out_shape=jax.ShapeDtypeStruct((batch, seq, d), F32),
        scratch_shapes=[pltpu.VMEM((tm, d), BF16)],
        compiler_params=pltpu.CompilerParams(dimension_semantics=("arbitrary", "arbitrary"),
                                             vmem_limit_bytes=VMEM_LIMIT),
        name="ffn",
    )(x, norm_g, w_gate, w_up, w_down, norm_f_g)


def kernel(x, norm1_g, w_in, gate_b, conv_w, conv_b, conv_ln_g, conv_ln_b, w_conv_out, w_attn_out,
           w_o, norm2_g, w_ffn_gate, w_ffn_up, w_ffn_down, norm_f_g):
    depth = w_in.shape[0]
    slopes = jnp.asarray(_alibi_slope_table())
    row = lambda a: a.reshape(1, -1)
    for l in range(depth):
        *qkvs, act, gates = _in_proj(x, row(norm1_g[l]), w_in[l].astype(BF16), row(gate_b[l]),
                                     conv_w[l], row(conv_b[l]), row(conv_ln_g[l]), row(conv_ln_b[l]))
        y_attn = _attention(qkvs, slopes)
        x = _merge(x, act, gates, y_attn, w_conv_out[l].astype(BF16), w_attn_out[l].astype(BF16),
                   w_o[l].astype(BF16))
        x = _ffn(x, row(norm2_g[l]), w_ffn_gate[l].astype(BF16), w_ffn_up[l].astype(BF16),
                 w_ffn_down[l].astype(BF16), row(norm_f_g), final_norm=(l == depth - 1))
    return x
```

```python
import functools
import math

import numpy as np
import jax
import jax.numpy as jnp
from jax import lax
from jax.experimental import pallas as pl
from jax.experimental.pallas import tpu as pltpu

D_MODEL = 1024
SEQ = 2048
HEAD_DIM = 64
GROUPS = ((128, 1), (512, 4), (2048, 16))
N_GROUPS = len(GROUPS)
HEADS_PER_GROUP = 8
GROUP_WIDTH = HEADS_PER_GROUP * HEAD_DIM
ATTN_WIDTH = N_GROUPS * GROUP_WIDTH
Q_BLOCK = 128
CONV_WIDTH = D_MODEL
CONV_KERNEL = 31
D_FF = 2816
RMS_EPS = 1e-6
LN_EPS = 1e-5
MASK_BIAS = -1e30

LANES = 128
SUBLANES = 8
HEADS_PER_STEP = LANES // HEAD_DIM
TOKEN_TILE = 512
CONV_HALO = 32
CONV_ROWS = 64
NORM_ROWS = 16
FIRST_BLOCK_BATCH = 2
FF_CHUNK = 512
VMEM_LIMIT = 56 * 1024 * 1024

U_OFF = 3 * ATTN_WIDTH
GATE_OFF = U_OFF + 2 * CONV_WIDTH

BF16 = jnp.bfloat16
F32 = jnp.float32


def _alibi_slope_table():
    def pow2(m):
        start = 2.0 ** (-8.0 / m)
        return [start ** (i + 1) for i in range(m)]

    def slopes(n):
        if math.log2(n).is_integer():
            return pow2(n)
        c = 2 ** math.floor(math.log2(n))
        return pow2(c) + slopes(2 * c)[0::2][: n - c]

    s = sorted(slopes(N_GROUPS * HEADS_PER_GROUP), reverse=True)
    return np.asarray(s, dtype=np.float32)


def _resident(shape):
    return pl.BlockSpec(shape, lambda *_: (0,) * len(shape), pipeline_mode=pl.Buffered(1))


def _sigmoid(z):
    return 1.0 / (1.0 + jnp.exp(-z))


def _row_pitch(dilation):
    return dilation + 1 if dilation % SUBLANES == 0 else dilation


def _in_proj_kernel(x_ref, g_ref, w_ref, gb_ref, cw_ref, cb_ref,
                    qkv0_ref, qkv1_ref, qkv2_ref, conv_ref, gate_ref,
                    hn_ref, h1_ref, h4_ref, h16_ref, cbuf_ref, *, tiles_per_seq):
    tm = x_ref.shape[1]
    n_slabs = cbuf_ref.shape[0]

    seq_start = pl.program_id(0) % tiles_per_seq == 0

    @pl.when(seq_start)
    def _():
        cbuf_ref[:, 0:CONV_HALO, :] = jnp.zeros((n_slabs, CONV_HALO, LANES), F32)

    @pl.when(jnp.logical_not(seq_start))
    def _():
        cbuf_ref[:, 0:CONV_HALO, :] = cbuf_ref[:, tm:tm + CONV_HALO, :]

    x = x_ref[0]
    hn = x * lax.rsqrt(jnp.mean(x * x, axis=-1, keepdims=True) + RMS_EPS) * g_ref[...]
    h1_ref[...] = hn.astype(BF16)
    for cb in range(hn_ref.shape[0]):
        hn_ref[cb] = hn[:, cb * LANES:(cb + 1) * LANES]
    for r, h_ref in ((GROUPS[1][1], h4_ref), (GROUPS[2][1], h16_ref)):
        rows = tm // r
        for c in range(r):
            for cb in range(hn_ref.shape[0]):
                h_ref[c * rows:(c + 1) * rows, cb * LANES:(cb + 1) * LANES] = (
                    hn_ref[cb, pl.ds(c, rows, stride=r), :].astype(BF16))

    h1 = h1_ref[...]
    slabs_per_dot = GROUP_WIDTH // LANES
    for jj in range(CONV_WIDTH // GROUP_WIDTH):
        lo = jj * GROUP_WIDTH
        ua = jnp.dot(h1, w_ref[:, U_OFF + lo:U_OFF + lo + GROUP_WIDTH], preferred_element_type=F32)
        ub = jnp.dot(h1, w_ref[:, U_OFF + CONV_WIDTH + lo:U_OFF + CONV_WIDTH + lo + GROUP_WIDTH],
                     preferred_element_type=F32)
        glu = ua * _sigmoid(ub)
        for s in range(slabs_per_dot):
            cbuf_ref[jj * slabs_per_dot + s, CONV_HALO:, :] = glu[:, s * LANES:(s + 1) * LANES]

    for row in range(0, tm, CONV_ROWS):
        for cb in range(n_slabs):
            cols = slice(cb * LANES, (cb + 1) * LANES)
            acc = jnp.broadcast_to(cb_ref[:, cols], (CONV_ROWS, LANES))
            for k in range(CONV_KERNEL):
                src = row + CONV_HALO - CONV_KERNEL + 1 + k
                acc = acc + cbuf_ref[cb, src:src + CONV_ROWS, :] * cw_ref[k:k + 1, cols]
            conv_ref[0, row:row + CONV_ROWS, cols] = acc

    for g, (h_ref, out_ref) in enumerate(((h1_ref, qkv0_ref), (h4_ref, qkv1_ref), (h16_ref, qkv2_ref))):
        r = GROUPS[g][1]
        rows = tm // r
        for which in range(3):
            col = which * ATTN_WIDTH + g * GROUP_WIDTH
            acc = jnp.dot(h_ref[...], w_ref[:, col:col + GROUP_WIDTH],
                          preferred_element_type=F32).astype(BF16)
            for c in range(r):
                out_ref[which, 0, c] = acc[c * rows:(c + 1) * rows]

    for jj in range(2 * D_MODEL // GROUP_WIDTH):
        lo = jj * GROUP_WIDTH
        gl = jnp.dot(h1, w_ref[:, GATE_OFF + lo:GATE_OFF + lo + GROUP_WIDTH], preferred_element_type=F32)
        gate_ref[0, :, lo:lo + GROUP_WIDTH] = _sigmoid(gl + gb_ref[:, lo:lo + GROUP_WIDTH]).astype(BF16)


def _in_proj(x, norm_g, w_in, gate_b, conv_w, conv_b):
    batch, seq, d = x.shape
    tm = TOKEN_TILE
    tpb = seq // tm
    in_width = w_in.shape[1]
    qkv_shapes, qkv_specs = [], []
    for _, r in GROUPS:
        qkv_shapes.append(jax.ShapeDtypeStruct((3, batch, r, seq // r, GROUP_WIDTH), BF16))
        qkv_specs.append(pl.BlockSpec((3, 1, r, tm // r, GROUP_WIDTH),
                                      lambda i: (0, i // tpb, 0, i % tpb, 0)))
    tile = lambda width: pl.BlockSpec((1, tm, width), lambda i: (i // tpb, i % tpb, 0))
    return pl.pallas_call(
        functools.partial(_in_proj_kernel, tiles_per_seq=tpb),
        grid=(batch * tpb,),
        in_specs=[tile(d), _resident((1, d)), _resident((d, in_width)), _resident((1, 2 * d)),
                  _resident((CONV_KERNEL, CONV_WIDTH)), _resident((1, CONV_WIDTH))],
        out_specs=qkv_specs + [tile(CONV_WIDTH), tile(2 * d)],
        out_shape=qkv_shapes + [jax.ShapeDtypeStruct((batch, seq, CONV_WIDTH), F32),
                                jax.ShapeDtypeStruct((batch, seq, 2 * d), BF16)],
        scratch_shapes=[pltpu.VMEM((d // LANES, tm, LANES), F32), pltpu.VMEM((tm, d), BF16),
                        pltpu.VMEM((tm, d), BF16), pltpu.VMEM((tm, d), BF16),
                        pltpu.VMEM((CONV_WIDTH // LANES, CONV_HALO + tm, LANES), F32)],
        compiler_params=pltpu.CompilerParams(dimension_semantics=("arbitrary",),
                                             vmem_limit_bytes=VMEM_LIMIT),
        name="in_proj",
    )(x, norm_g, w_in, gate_b, conv_w, conv_b)


def _attn_kernel(slope_ref, qkv0_ref, qkv1_ref, qkv2_ref, y_ref, bias_ref, acc_ref, m_ref, l_ref):
    pair = pl.program_id(1)
    qb = Q_BLOCK
    lane = lax.broadcasted_iota(jnp.int32, (qb, LANES), 1)
    first_head = lane < HEAD_DIM
    scale = HEAD_DIM ** -0.5
    head_scale = [jnp.where(first_head, scale, 0.0).astype(BF16),
                  jnp.where(first_head, 0.0, scale).astype(BF16)]

    qi = lax.broadcasted_iota(jnp.int32, (qb, 2 * qb), 0)
    kj = lax.broadcasted_iota(jnp.int32, (qb, 2 * qb), 1)
    rel = qb + qi - kj
    valid = (rel >= 0) & (rel <= qb)
    rel_f = rel.astype(F32)
    for g, (_, r) in enumerate(GROUPS):
        for h in range(HEADS_PER_STEP):
            slope = slope_ref[g * HEADS_PER_GROUP + pair * HEADS_PER_STEP + h]
            bias_ref[g, h * qb:(h + 1) * qb, :] = jnp.where(valid, -(slope * r) * rel_f, MASK_BIAS)

    def blocks(g, qkv_ref, entries, with_prev):
        r = GROUPS[g][1]
        pitch = _row_pitch(r)
        bias = bias_ref[g] if with_prev else bias_ref[g, :, qb:]
        scores, values = [], []
        for c, n in entries:
            row = n * qb
            keys = pl.ds(row - qb, 2 * qb) if with_prev else pl.ds(row, qb)
            q = qkv_ref[0, 0, c, pl.ds(row, qb), :]
            k = qkv_ref[1, 0, c, keys, :]
            v = qkv_ref[2, 0, c, keys, :]
            q2 = jnp.concatenate([q * head_scale[0], q * head_scale[1]], axis=0)
            scores.append(lax.dot_general(q2, k, (((1,), (1,)), ((), ())),
                                          preferred_element_type=F32) + bias)
            values.append(jnp.concatenate([v, jnp.ones_like(v)], axis=1))
        s = jnp.concatenate(scores, axis=0)
        m = jnp.max(s, axis=-1, keepdims=True)
        p = jnp.exp(s - m).astype(BF16)
        for i, (c, n) in enumerate(entries):
            lo = i * 2 * qb
            pv = jnp.dot(p[lo:lo + 2 * qb], values[i], preferred_element_type=F32)
            tok = n * (qb * pitch) + c
            rows = pl.ds(tok, qb, stride=pitch) if r > 1 else pl.ds(tok, qb)
            acc_ref[g, rows, :] = jnp.where(first_head, pv[:qb, :LANES], pv[qb:, :LANES])
            l_ref[g, rows, :] = jnp.where(first_head, pv[:qb, LANES:], pv[qb:, LANES:])
            m_ref[g, rows, :] = jnp.where(first_head, m[lo:lo + qb], m[lo + qb:lo + 2 * qb])

    for g, qkv_ref in enumerate((qkv0_ref, qkv1_ref, qkv2_ref)):
        r = GROUPS[g][1]
        first = [(c, 0) for c in range(r)]
        later = [(c, n) for c in range(r) for n in range(1, SEQ // r // qb)]
        for i in range(0, len(first), FIRST_BLOCK_BATCH):
            blocks(g, qkv_ref, first[i:i + FIRST_BLOCK_BATCH], False)
        for entry in later:
            blocks(g, qkv_ref, [entry], True)

    def mix(i, carry):
        chunk = 2 * qb
        rows = pl.ds(pl.multiple_of(i * chunk, chunk), chunk)

        def tokens(ref, g):
            r = GROUPS[g][1]
            pitch = _row_pitch(r)
            if pitch == r:
                return ref[g, rows, :]
            first = i * (chunk // r)
            return jnp.concatenate([ref[g, pl.ds((first + j) * pitch, r), :]
                                    for j in range(chunk // r)], axis=0)

        ms = [tokens(m_ref, g) for g in range(N_GROUPS)]
        top = jnp.maximum(jnp.maximum(ms[0], ms[1]), ms[2])
        num = jnp.zeros_like(top)
        den = jnp.zeros_like(top)
        for g in range(N_GROUPS):
            w = jnp.exp(ms[g] - top)
            num = num + w * tokens(acc_ref, g)
            den = den + w * tokens(l_ref, g)
        y_ref[0, rows, :] = (num / den).astype(BF16)
        return carry
    lax.fori_loop(0, SEQ // (2 * qb), mix, 0)


def _attention(qkvs, slopes):
    batch = qkvs[0].shape[1]
    pairs = GROUP_WIDTH // LANES
    in_specs = [pl.BlockSpec(memory_space=pltpu.SMEM)]
    for _, r in GROUPS:
        in_specs.append(pl.BlockSpec((3, 1, r, SEQ // r, LANES), lambda b, j: (0, b, 0, 0, j)))
    result_rows = max((SEQ // r) * _row_pitch(r) for _, r in GROUPS)
    result_buf = pltpu.VMEM((N_GROUPS, result_rows, LANES), F32)
    return pl.pallas_call(
        _attn_kernel,
        grid=(batch, pairs),
        in_specs=in_specs,
        out_specs=pl.BlockSpec((1, SEQ, LANES), lambda b, j: (b, 0, j)),
        out_shape=jax.ShapeDtypeStruct((batch, SEQ, GROUP_WIDTH), BF16),
        scratch_shapes=[pltpu.VMEM((N_GROUPS, HEADS_PER_STEP * Q_BLOCK, 2 * Q_BLOCK), F32),
                        result_buf, result_buf, result_buf],
        compiler_params=pltpu.CompilerParams(dimension_semantics=("arbitrary", "arbitrary"),
                                             vmem_limit_bytes=VMEM_LIMIT),
        name="attn",
    )(slopes, *qkvs)


def _merge_ffn_kernel(x_ref, conv_ref, gate_ref, ya_ref, lg_ref, lb_ref, wc_ref, wa_ref, wo_ref,
                      g2_ref, wg_ref, wu_ref, wd_ref, gf_ref, o_ref, act_ref, h_ref, *, final_norm):
    tm = x_ref.shape[1]
    for row in range(0, tm, NORM_ROWS):
        c = conv_ref[0, row:row + NORM_ROWS, :]
        mu = jnp.mean(c, axis=-1, keepdims=True)
        d = c - mu
        var = jnp.mean(d * d, axis=-1, keepdims=True)
        y = d * lax.rsqrt(var + LN_EPS) * lg_ref[...] + lb_ref[...]
        act_ref[row:row + NORM_ROWS, :] = (y * _sigmoid(y)).astype(BF16)

    y_attn = jnp.dot(ya_ref[0], wa_ref[...], preferred_element_type=F32)
    y_conv = jnp.dot(act_ref[...], wc_ref[...], preferred_element_type=F32)
    g_attn = gate_ref[0, :, :D_MODEL].astype(F32)
    g_conv = gate_ref[0, :, D_MODEL:].astype(F32)
    merged = (g_attn * y_attn + g_conv * y_conv).astype(BF16)
    x = x_ref[0] + jnp.dot(merged, wo_ref[...], preferred_element_type=F32)

    h = x * lax.rsqrt(jnp.mean(x * x, axis=-1, keepdims=True) + RMS_EPS) * g2_ref[...]
    h_ref[...] = h.astype(BF16)
    out = x
    for lo in range(0, D_FF, FF_CHUNK):
        hi = min(lo + FF_CHUNK, D_FF)
        gate = jnp.dot(h_ref[...], wg_ref[:, lo:hi], preferred_element_type=F32)
        up = jnp.dot(h_ref[...], wu_ref[:, lo:hi], preferred_element_type=F32)
        act = (gate * _sigmoid(gate) * up).astype(BF16)
        out = out + jnp.dot(act, wd_ref[lo:hi, :], preferred_element_type=F32)
    if final_norm:
        out = out * lax.rsqrt(jnp.mean(out * out, axis=-1, keepdims=True) + RMS_EPS) * gf_ref[...]
    o_ref[0] = out


def _merge_ffn(x, conv, gates, y_attn, ln_g, ln_b, w_conv_out, w_attn_out, w_o,
               norm_g, w_gate, w_up, w_down, norm_f_g, final_norm):
    batch, seq, d = x.shape
    tm = TOKEN_TILE
    tile = lambda width: pl.BlockSpec((1, tm, width), lambda b, i: (b, i, 0))
    return pl.pallas_call(
        functools.partial(_merge_ffn_kernel, final_norm=final_norm),
        grid=(batch, seq // tm),
        in_specs=[tile(d), tile(CONV_WIDTH), tile(2 * d), tile(GROUP_WIDTH),
                  _resident((1, CONV_WIDTH)), _resident((1, CONV_WIDTH)),
                  _resident((CONV_WIDTH, d)), _resident((GROUP_WIDTH, d)), _resident((d, d)),
                  _resident((1, d)), _resident((d, D_FF)), _resident((d, D_FF)),
                  _resident((D_FF, d)), _resident((1, d))],
        out_specs=tile(d),
        out_shape=jax.ShapeDtypeStruct((batch, seq, d), F32),
        scratch_shapes=[pltpu.VMEM((tm, CONV_WIDTH), BF16), pltpu.VMEM((tm, d), BF16)],
        compiler_params=pltpu.CompilerParams(dimension_semantics=("arbitrary", "arbitrary"),
                                             vmem_limit_bytes=VMEM_LIMIT),
        name="merge_ffn",
    )(x, conv, gates, y_attn, ln_g, ln_b, w_conv_out, w_attn_out, w_o,
      norm_g, w_gate, w_up, w_down, norm_f_g)


def kernel(x, norm1_g, w_in, gate_b, conv_w, conv_b, conv_ln_g, conv_ln_b, w_conv_out, w_attn_out,
           w_o, norm2_g, w_ffn_gate, w_ffn_up, w_ffn_down, norm_f_g):
    depth = w_in.shape[0]
    slopes = jnp.asarray(_alibi_slope_table())
    row = lambda a: a.reshape(1, -1)
    for l in range(depth):
        *qkvs, conv, gates = _in_proj(x, row(norm1_g[l]), w_in[l].astype(BF16), row(gate_b[l]),
                                      conv_w[l], row(conv_b[l]))
        y_attn = _attention(qkvs, slopes)
        x = _merge_ffn(x, conv, gates, y_attn, row(conv_ln_g[l]), row(conv_ln_b[l]),
                       w_conv_out[l].astype(BF16), w_attn_out[l].astype(BF16), w_o[l].astype(BF16),
                       row(norm2_g[l]), w_ffn_gate[l].astype(BF16), w_ffn_up[l].astype(BF16),
                       w_ffn_down[l].astype(BF16), row(norm_f_g), final_norm=(l == depth - 1))
    return x
```

```python
import functools
import math

import numpy as np
import jax
import jax.numpy as jnp
from jax import lax
from jax.experimental import pallas as pl
from jax.experimental.pallas import tpu as pltpu

D_MODEL = 1024
SEQ = 2048
HEAD_DIM = 64
GROUPS = ((128, 1), (512, 4), (2048, 16))
N_GROUPS = len(GROUPS)
HEADS_PER_GROUP = 8
GROUP_WIDTH = HEADS_PER_GROUP * HEAD_DIM
ATTN_WIDTH = N_GROUPS * GROUP_WIDTH
Q_BLOCK = 128
CONV_WIDTH = D_MODEL
CONV_KERNEL = 31
D_FF = 2816
RMS_EPS = 1e-6
LN_EPS = 1e-5
MASK_BIAS = -1e30

LANES = 128
SUBLANES = 8
HEADS_PER_STEP = LANES // HEAD_DIM
TOKEN_TILE = 512
CONV_HALO = 32
CONV_ROWS = 16
NORM_ROWS = 16
FIRST_BLOCK_BATCH = 2
FF_CHUNK = 512
VMEM_LIMIT = 56 * 1024 * 1024

U_OFF = 3 * ATTN_WIDTH
GATE_OFF = U_OFF + 2 * CONV_WIDTH

BF16 = jnp.bfloat16
F32 = jnp.float32


def _alibi_slope_table():
    def pow2(m):
        start = 2.0 ** (-8.0 / m)
        return [start ** (i + 1) for i in range(m)]

    def slopes(n):
        if math.log2(n).is_integer():
            return pow2(n)
        c = 2 ** math.floor(math.log2(n))
        return pow2(c) + slopes(2 * c)[0::2][: n - c]

    s = sorted(slopes(N_GROUPS * HEADS_PER_GROUP), reverse=True)
    return np.asarray(s, dtype=np.float32)


def _resident(shape):
    return pl.BlockSpec(shape, lambda *_: (0,) * len(shape), pipeline_mode=pl.Buffered(1))


def _sigmoid(z):
    return 1.0 / (1.0 + jnp.exp(-z))


def _row_pitch(dilation):
    return dilation + 1 if dilation % SUBLANES == 0 else dilation


def _in_proj_kernel(x_ref, g_ref, w_ref, gb_ref, cw_ref, cb_ref,
                    qkv0_ref, gate_ref, conv_ref, qkv1_ref, qkv2_ref,
                    hn_ref, h1_ref, h4_ref, h16_ref, cbuf_ref, *, tiles_per_seq):
    tm = x_ref.shape[0]
    n_slabs = cbuf_ref.shape[0]

    seq_start = pl.program_id(0) % tiles_per_seq == 0

    @pl.when(seq_start)
    def _():
        cbuf_ref[:, 0:CONV_HALO, :] = jnp.zeros((n_slabs, CONV_HALO, LANES), F32)

    @pl.when(jnp.logical_not(seq_start))
    def _():
        cbuf_ref[:, 0:CONV_HALO, :] = cbuf_ref[:, tm:tm + CONV_HALO, :]

    x = x_ref[...]
    hn = x * lax.rsqrt(jnp.mean(x * x, axis=-1, keepdims=True) + RMS_EPS) * g_ref[...]
    h1_ref[...] = hn.astype(BF16)
    for cb in range(hn_ref.shape[0]):
        hn_ref[cb] = hn[:, cb * LANES:(cb + 1) * LANES]
    for r, h_ref in ((GROUPS[1][1], h4_ref), (GROUPS[2][1], h16_ref)):
        rows = tm // r
        for c in range(r):
            for cb in range(hn_ref.shape[0]):
                h_ref[c * rows:(c + 1) * rows, cb * LANES:(cb + 1) * LANES] = (
                    hn_ref[cb, pl.ds(c, rows, stride=r), :].astype(BF16))

    h1 = h1_ref[...]
    slabs_per_dot = GROUP_WIDTH // LANES
    for jj in range(CONV_WIDTH // GROUP_WIDTH):
        lo = jj * GROUP_WIDTH
        ua = jnp.dot(h1, w_ref[:, U_OFF + lo:U_OFF + lo + GROUP_WIDTH], preferred_element_type=F32)
        ub = jnp.dot(h1, w_ref[:, U_OFF + CONV_WIDTH + lo:U_OFF + CONV_WIDTH + lo + GROUP_WIDTH],
                     preferred_element_type=F32)
        glu = ua * _sigmoid(ub)
        for s in range(slabs_per_dot):
            cbuf_ref[jj * slabs_per_dot + s, CONV_HALO:, :] = glu[:, s * LANES:(s + 1) * LANES]

    for g, (h_ref, out_ref) in enumerate(((h1_ref, qkv0_ref), (h4_ref, qkv1_ref), (h16_ref, qkv2_ref))):
        r = GROUPS[g][1]
        rows = tm // r
        for which in range(3):
            col = which * ATTN_WIDTH + g * GROUP_WIDTH
            acc = jnp.dot(h_ref[...], w_ref[:, col:col + GROUP_WIDTH],
                          preferred_element_type=F32).astype(BF16)
            if r == 1:
                out_ref[:, which * GROUP_WIDTH:(which + 1) * GROUP_WIDTH] = acc
            else:
                for c in range(r):
                    out_ref[which, c] = acc[c * rows:(c + 1) * rows]

    for jj in range(2 * D_MODEL // GROUP_WIDTH):
        lo = jj * GROUP_WIDTH
        gl = jnp.dot(h1, w_ref[:, GATE_OFF + lo:GATE_OFF + lo + GROUP_WIDTH], preferred_element_type=F32)
        gate_ref[:, lo:lo + GROUP_WIDTH] = _sigmoid(gl + gb_ref[:, lo:lo + GROUP_WIDTH]).astype(BF16)

    for row in range(0, tm, CONV_ROWS):
        for cb in range(n_slabs):
            cols = slice(cb * LANES, (cb + 1) * LANES)
            acc = jnp.broadcast_to(cb_ref[:, cols], (CONV_ROWS, LANES))
            for k in range(CONV_KERNEL):
                src = row + CONV_HALO - CONV_KERNEL + 1 + k
                acc = acc + cbuf_ref[cb, src:src + CONV_ROWS, :] * cw_ref[k:k + 1, cols]
            conv_ref[row:row + CONV_ROWS, cols] = acc


def _in_proj(x, norm_g, w_in, gate_b, conv_w, conv_b):
    batch, seq, d = x.shape
    tm = TOKEN_TILE
    tpb = seq // tm
    in_width = w_in.shape[1]
    tile = lambda width: pl.BlockSpec((tm, width), lambda i: (i, 0))
    qkv_shapes = [jax.ShapeDtypeStruct((batch * seq, 3 * GROUP_WIDTH), BF16)]
    qkv_specs = [tile(3 * GROUP_WIDTH)]
    for _, r in GROUPS[1:]:
        qkv_shapes.append(jax.ShapeDtypeStruct((3, batch * r, seq // r, GROUP_WIDTH), BF16))
        qkv_specs.append(pl.BlockSpec((3, r, tm // r, GROUP_WIDTH),
                                      lambda i: (0, i // tpb, i % tpb, 0)))
    return pl.pallas_call(
        functools.partial(_in_proj_kernel, tiles_per_seq=tpb),
        grid=(batch * tpb,),
        in_specs=[tile(d), _resident((1, d)), _resident((d, in_width)), _resident((1, 2 * d)),
                  _resident((CONV_KERNEL, CONV_WIDTH)), _resident((1, CONV_WIDTH))],
        out_specs=[qkv_specs[0], tile(2 * d), tile(CONV_WIDTH)] + qkv_specs[1:],
        out_shape=[qkv_shapes[0], jax.ShapeDtypeStruct((batch * seq, 2 * d), BF16),
                   jax.ShapeDtypeStruct((batch * seq, CONV_WIDTH), F32)] + qkv_shapes[1:],
        scratch_shapes=[pltpu.VMEM((d // LANES, tm, LANES), F32), pltpu.VMEM((tm, d), BF16),
                        pltpu.VMEM((tm, d), BF16), pltpu.VMEM((tm, d), BF16),
                        pltpu.VMEM((CONV_WIDTH // LANES, CONV_HALO + tm, LANES), F32)],
        compiler_params=pltpu.CompilerParams(dimension_semantics=("arbitrary",),
                                             vmem_limit_bytes=VMEM_LIMIT),
        name="in_proj",
    )(x.reshape(batch * seq, d), norm_g, w_in, gate_b, conv_w, conv_b)


def _attn_kernel(slope_ref, q0_ref, k0_ref, v0_ref, qkv1_ref, qkv2_ref, y_ref,
                 bias_ref, acc_ref, m_ref, l_ref):
    pair = pl.program_id(1)
    dilated = (None, qkv1_ref, qkv2_ref)

    def load(g, which, c, rows):
        if g == 0:
            return (q0_ref, k0_ref, v0_ref)[which][rows, :]
        return dilated[g][which, c, rows, :]

    qb = Q_BLOCK
    lane = lax.broadcasted_iota(jnp.int32, (qb, LANES), 1)
    first_head = lane < HEAD_DIM
    scale = HEAD_DIM ** -0.5
    head_scale = [jnp.where(first_head, scale, 0.0).astype(BF16),
                  jnp.where(first_head, 0.0, scale).astype(BF16)]

    qi = lax.broadcasted_iota(jnp.int32, (qb, 2 * qb), 0)
    kj = lax.broadcasted_iota(jnp.int32, (qb, 2 * qb), 1)
    rel = qb + qi - kj
    valid = (rel >= 0) & (rel <= qb)
    rel_f = rel.astype(F32)
    for g, (_, r) in enumerate(GROUPS):
        for h in range(HEADS_PER_STEP):
            slope = slope_ref[g * HEADS_PER_GROUP + pair * HEADS_PER_STEP + h]
            bias_ref[g, h * qb:(h + 1) * qb, :] = jnp.where(valid, -(slope * r) * rel_f, MASK_BIAS)

    def blocks(g, entries, with_prev):
        r = GROUPS[g][1]
        pitch = _row_pitch(r)
        bias = bias_ref[g] if with_prev else bias_ref[g, :, qb:]
        scores, values = [], []
        for c, n in entries:
            row = n * qb
            keys = pl.ds(row - qb, 2 * qb) if with_prev else pl.ds(row, qb)
            q = load(g, 0, c, pl.ds(row, qb))
            k = load(g, 1, c, keys)
            v = load(g, 2, c, keys)
            q2 = jnp.concatenate([q * head_scale[0], q * head_scale[1]], axis=0)
            scores.append(lax.dot_general(q2, k, (((1,), (1,)), ((), ())),
                                          preferred_element_type=F32) + bias)
            values.append(jnp.concatenate([v, jnp.ones_like(v)], axis=1))
        s = jnp.concatenate(scores, axis=0)
        m = jnp.max(s, axis=-1, keepdims=True)
        p = jnp.exp(s - m).astype(BF16)
        for i, (c, n) in enumerate(entries):
            lo = i * 2 * qb
            pv = jnp.dot(p[lo:lo + 2 * qb], values[i], preferred_element_type=F32)
            tok = n * (qb * pitch) + c
            rows = pl.ds(tok, qb, stride=pitch) if r > 1 else pl.ds(tok, qb)
            acc_ref[g, rows, :] = jnp.where(first_head, pv[:qb, :LANES], pv[qb:, :LANES])
            l_ref[g, rows, :] = jnp.where(first_head, pv[:qb, LANES:], pv[qb:, LANES:])
            m_ref[g, rows, :] = jnp.where(first_head, m[lo:lo + qb], m[lo + qb:lo + 2 * qb])

    for g, (_, r) in enumerate(GROUPS):
        first = [(c, 0) for c in range(r)]
        later = [(c, n) for c in range(r) for n in range(1, SEQ // r // qb)]
        for i in range(0, len(first), FIRST_BLOCK_BATCH):
            blocks(g, first[i:i + FIRST_BLOCK_BATCH], False)
        for entry in later:
            blocks(g, [entry], True)

    def mix(i, carry):
        chunk = 2 * qb
        rows = pl.ds(pl.multiple_of(i * chunk, chunk), chunk)

        def tokens(ref, g):
            r = GROUPS[g][1]
            pitch = _row_pitch(r)
            if pitch == r:
                return ref[g, rows, :]
            first = i * (chunk // r)
            return jnp.concatenate([ref[g, pl.ds((first + j) * pitch, r), :]
                                    for j in range(chunk // r)], axis=0)

        ms = [tokens(m_ref, g) for g in range(N_GROUPS)]
        top = jnp.maximum(jnp.maximum(ms[0], ms[1]), ms[2])
        num = jnp.zeros_like(top)
        den = jnp.zeros_like(top)
        for g in range(N_GROUPS):
            w = jnp.exp(ms[g] - top)
            num = num + w * tokens(acc_ref, g)
            den = den + w * tokens(l_ref, g)
        y_ref[0, rows, :] = (num / den).astype(BF16)
        return carry
    lax.fori_loop(0, SEQ // (2 * qb), mix, 0)


def _attention(qkvs, slopes):
    batch = qkvs[0].shape[0] // SEQ
    pairs = GROUP_WIDTH // LANES
    in_specs = [pl.BlockSpec(memory_space=pltpu.SMEM)]
    for which in range(3):
        in_specs.append(pl.BlockSpec((SEQ, LANES), lambda b, j, which=which: (b, which * pairs + j)))
    for _, r in GROUPS[1:]:
        in_specs.append(pl.BlockSpec((3, r, SEQ // r, LANES), lambda b, j: (0, b, 0, j)))
    result_rows = max((SEQ // r) * _row_pitch(r) for _, r in GROUPS)
    result_buf = pltpu.VMEM((N_GROUPS, result_rows, LANES), F32)
    return pl.pallas_call(
        _attn_kernel,
        grid=(batch, pairs),
        in_specs=in_specs,
        out_specs=pl.BlockSpec((1, SEQ, LANES), lambda b, j: (b, 0, j)),
        out_shape=jax.ShapeDtypeStruct((batch, SEQ, GROUP_WIDTH), BF16),
        scratch_shapes=[pltpu.VMEM((N_GROUPS, HEADS_PER_STEP * Q_BLOCK, 2 * Q_BLOCK), F32),
                        result_buf, result_buf, result_buf],
        compiler_params=pltpu.CompilerParams(dimension_semantics=("arbitrary", "arbitrary"),
                                             vmem_limit_bytes=VMEM_LIMIT),
        name="attn",
    )(slopes, qkvs[0], qkvs[0], qkvs[0], *qkvs[1:])


def _merge_ffn_kernel(x_ref, conv_ref, gate_ref, ya_ref, lg_ref, lb_ref, wc_ref, wa_ref, wo_ref,
                      g2_ref, wg_ref, wu_ref, wd_ref, gf_ref, o_ref, act_ref, h_ref, *, final_norm):
    tm = x_ref.shape[1]
    for row in range(0, tm, NORM_ROWS):
        c = conv_ref[0, row:row + NORM_ROWS, :]
        mu = jnp.mean(c, axis=-1, keepdims=True)
        d = c - mu
        var = jnp.mean(d * d, axis=-1, keepdims=True)
        y = d * lax.rsqrt(var + LN_EPS) * lg_ref[...] + lb_ref[...]
        act_ref[row:row + NORM_ROWS, :] = (y * _sigmoid(y)).astype(BF16)

    y_attn = jnp.dot(ya_ref[0], wa_ref[...], preferred_element_type=F32)
    y_conv = jnp.dot(act_ref[...], wc_ref[...], preferred_element_type=F32)
    g_attn = gate_ref[0, :, :D_MODEL].astype(F32)
    g_conv = gate_ref[0, :, D_MODEL:].astype(F32)
    merged = (g_attn * y_attn + g_conv * y_conv).astype(BF16)
    x = x_ref[0] + jnp.dot(merged, wo_ref[...], preferred_element_type=F32)

    h = x * lax.rsqrt(jnp.mean(x * x, axis=-1, keepdims=True) + RMS_EPS) * g2_ref[...]
    h_ref[...] = h.astype(BF16)
    out = x
    for lo in range(0, D_FF, FF_CHUNK):
        hi = min(lo + FF_CHUNK, D_FF)
        gate = jnp.dot(h_ref[...], wg_ref[:, lo:hi], preferred_element_type=F32)
        up = jnp.dot(h_ref[...], wu_ref[:, lo:hi], preferred_element_type=F32)
        act = (gate * _sigmoid(gate) * up).astype(BF16)
        out = out + jnp.dot(act, wd_ref[lo:hi, :], preferred_element_type=F32)
    if final_norm:
        out = out * lax.rsqrt(jnp.mean(out * out, axis=-1, keepdims=True) + RMS_EPS) * gf_ref[...]
    o_ref[0] = out


def _merge_ffn(x, conv, gates, y_attn, ln_g, ln_b, w_conv_out, w_attn_out, w_o,
               norm_g, w_gate, w_up, w_down, norm_f_g, final_norm):
    batch, seq, d = x.shape
    tm = TOKEN_TILE
    tile = lambda width: pl.BlockSpec((1, tm, width), lambda b, i: (b, i, 0))
    return pl.pallas_call(
        functools.partial(_merge_ffn_kernel, final_norm=final_norm),
        grid=(batch, seq // tm),
        in_specs=[tile(d), tile(CONV_WIDTH), tile(2 * d), tile(GROUP_WIDTH),
                  _resident((1, CONV_WIDTH)), _resident((1, CONV_WIDTH)),
                  _resident((CONV_WIDTH, d)), _resident((GROUP_WIDTH, d)), _resident((d, d)),
                  _resident((1, d)), _resident((d, D_FF)), _resident((d, D_FF)),
                  _resident((D_FF, d)), _resident((1, d))],
        out_specs=tile(d),
        out_shape=jax.ShapeDtypeStruct((batch, seq, d), F32),
        scratch_shapes=[pltpu.VMEM((tm, CONV_WIDTH), BF16), pltpu.VMEM((tm, d), BF16)],
        compiler_params=pltpu.CompilerParams(dimension_semantics=("arbitrary", "arbitrary"),
                                             vmem_limit_bytes=VMEM_LIMIT),
        name="merge_ffn",
    )(x, conv, gates, y_attn, ln_g, ln_b, w_conv_out, w_attn_out, w_o,
      norm_g, w_gate, w_up, w_down, norm_f_g)


def kernel(x, norm1_g, w_in, gate_b, conv_w, conv_b, conv_ln_g, conv_ln_b, w_conv_out, w_attn_out,
           w_o, norm2_g, w_ffn_gate, w_ffn_up, w_ffn_down, norm_f_g):
    depth = w_in.shape[0]
    slopes = jnp.asarray(_alibi_slope_table())
    row = lambda a: a.reshape(1, -1)
    for l in range(depth):
        qkv0, gates, conv, *dilated = _in_proj(x, row(norm1_g[l]), w_in[l].astype(BF16),
                                               row(gate_b[l]), conv_w[l], row(conv_b[l]))
        y_attn = _attention([qkv0] + dilated, slopes)
        conv = conv.reshape(x.shape[0], x.shape[1], -1)
        gates = gates.reshape(x.shape[0], x.shape[1], -1)
        x = _merge_ffn(x, conv, gates, y_attn, row(conv_ln_g[l]), row(conv_ln_b[l]),
                       w_conv_out[l].astype(BF16), w_attn_out[l].astype(BF16), w_o[l].astype(BF16),
                       row(norm2_g[l]), w_ffn_gate[l].astype(BF16), w_ffn_up[l].astype(BF16),
                       w_ffn_down[l].astype(BF16), row(norm_f_g), final_norm=(l == depth - 1))
    return x
```

```python
import functools
import math

import numpy as np
import jax
import jax.numpy as jnp
from jax import lax
from jax.experimental import pallas as pl
from jax.experimental.pallas import tpu as pltpu

D_MODEL = 1024
SEQ = 2048
HEAD_DIM = 64
GROUPS = ((128, 1), (512, 4), (2048, 16))
N_GROUPS = len(GROUPS)
HEADS_PER_GROUP = 8
GROUP_WIDTH = HEADS_PER_GROUP * HEAD_DIM
ATTN_WIDTH = N_GROUPS * GROUP_WIDTH
Q_BLOCK = 128
CONV_WIDTH = D_MODEL
CONV_KERNEL = 31
D_FF = 2816
RMS_EPS = 1e-6
LN_EPS = 1e-5
MASK_BIAS = -1e30

LANES = 128
SUBLANES = 8
HEADS_PER_STEP = LANES // HEAD_DIM
TOKEN_TILE = 512
CONV_HALO = 32
CONV_ROWS = 16
NORM_ROWS = 16
FIRST_BLOCK_BATCH = 2
FF_CHUNK = 512
VMEM_LIMIT = 56 * 1024 * 1024

U_OFF = 3 * ATTN_WIDTH
GATE_OFF = U_OFF + 2 * CONV_WIDTH

BF16 = jnp.bfloat16
F32 = jnp.float32


def _alibi_slope_table():
    def pow2(m):
        start = 2.0 ** (-8.0 / m)
        return [start ** (i + 1) for i in range(m)]

    def slopes(n):
        if math.log2(n).is_integer():
            return pow2(n)
        c = 2 ** math.floor(math.log2(n))
        return pow2(c) + slopes(2 * c)[0::2][: n - c]

    s = sorted(slopes(N_GROUPS * HEADS_PER_GROUP), reverse=True)
    return np.asarray(s, dtype=np.float32)


def _resident(shape):
    return pl.BlockSpec(shape, lambda *_: (0,) * len(shape), pipeline_mode=pl.Buffered(1))


def _sigmoid(z):
    return 0.5 * jnp.tanh(0.5 * z) + 0.5


def _row_pitch(dilation):
    return dilation + 1 if dilation % SUBLANES == 0 else dilation


def _in_proj_kernel(x_ref, g_ref, w_ref, gb_ref, cw_ref, cb_ref,
                    qkv0_ref, gate_ref, conv_ref, qkv1_ref, qkv2_ref,
                    hn_ref, h1_ref, h4_ref, h16_ref, cbuf_ref, *, tiles_per_seq):
    tm = x_ref.shape[0]
    n_slabs = cbuf_ref.shape[0]

    seq_start = pl.program_id(0) % tiles_per_seq == 0

    @pl.when(seq_start)
    def _():
        cbuf_ref[:, 0:CONV_HALO, :] = jnp.zeros((n_slabs, CONV_HALO, LANES), F32)

    @pl.when(jnp.logical_not(seq_start))
    def _():
        cbuf_ref[:, 0:CONV_HALO, :] = cbuf_ref[:, tm:tm + CONV_HALO, :]

    x = x_ref[...]
    hn = x * lax.rsqrt(jnp.mean(x * x, axis=-1, keepdims=True) + RMS_EPS) * g_ref[...]
    h1_ref[...] = hn.astype(BF16)
    for cb in range(hn_ref.shape[0]):
        hn_ref[cb] = hn[:, cb * LANES:(cb + 1) * LANES]
    for r, h_ref in ((GROUPS[1][1], h4_ref), (GROUPS[2][1], h16_ref)):
        rows = tm // r
        for c in range(r):
            for cb in range(hn_ref.shape[0]):
                h_ref[c * rows:(c + 1) * rows, cb * LANES:(cb + 1) * LANES] = (
                    hn_ref[cb, pl.ds(c, rows, stride=r), :].astype(BF16))

    h1 = h1_ref[...]
    slabs_per_dot = GROUP_WIDTH // LANES
    for jj in range(CONV_WIDTH // GROUP_WIDTH):
        lo = jj * GROUP_WIDTH
        ua = jnp.dot(h1, w_ref[:, U_OFF + lo:U_OFF + lo + GROUP_WIDTH], preferred_element_type=F32)
        ub = jnp.dot(h1, w_ref[:, U_OFF + CONV_WIDTH + lo:U_OFF + CONV_WIDTH + lo + GROUP_WIDTH],
                     preferred_element_type=F32)
        glu = ua * _sigmoid(ub)
        for s in range(slabs_per_dot):
            cbuf_ref[jj * slabs_per_dot + s, CONV_HALO:, :] = glu[:, s * LANES:(s + 1) * LANES]

    for g, (h_ref, out_ref) in enumerate(((h1_ref, qkv0_ref), (h4_ref, qkv1_ref), (h16_ref, qkv2_ref))):
        r = GROUPS[g][1]
        rows = tm // r
        for which in range(3):
            col = which * ATTN_WIDTH + g * GROUP_WIDTH
            acc = jnp.dot(h_ref[...], w_ref[:, col:col + GROUP_WIDTH],
                          preferred_element_type=F32).astype(BF16)
            if r == 1:
                out_ref[:, which * GROUP_WIDTH:(which + 1) * GROUP_WIDTH] = acc
            else:
                for c in range(r):
                    out_ref[which, c] = acc[c * rows:(c + 1) * rows]

    for jj in range(2 * D_MODEL // GROUP_WIDTH):
        lo = jj * GROUP_WIDTH
        gl = jnp.dot(h1, w_ref[:, GATE_OFF + lo:GATE_OFF + lo + GROUP_WIDTH], preferred_element_type=F32)
        gate_ref[:, lo:lo + GROUP_WIDTH] = _sigmoid(gl + gb_ref[:, lo:lo + GROUP_WIDTH]).astype(BF16)

    for row in range(0, tm, CONV_ROWS):
        for cb in range(n_slabs):
            cols = slice(cb * LANES, (cb + 1) * LANES)
            acc = jnp.broadcast_to(cb_ref[:, cols], (CONV_ROWS, LANES))
            for k in range(CONV_KERNEL):
                src = row + CONV_HALO - CONV_KERNEL + 1 + k
                acc = acc + cbuf_ref[cb, src:src + CONV_ROWS, :] * cw_ref[k:k + 1, cols]
            conv_ref[row:row + CONV_ROWS, cols] = acc


def _in_proj(x, norm_g, w_in, gate_b, conv_w, conv_b):
    batch, seq, d = x.shape
    tm = TOKEN_TILE
    tpb = seq // tm
    in_width = w_in.shape[1]
    tile = lambda width: pl.BlockSpec((tm, width), lambda i: (i, 0))
    qkv_shapes = [jax.ShapeDtypeStruct((batch * seq, 3 * GROUP_WIDTH), BF16)]
    qkv_specs = [tile(3 * GROUP_WIDTH)]
    for _, r in GROUPS[1:]:
        qkv_shapes.append(jax.ShapeDtypeStruct((3, batch * r, seq // r, GROUP_WIDTH), BF16))
        qkv_specs.append(pl.BlockSpec((3, r, tm // r, GROUP_WIDTH),
                                      lambda i: (0, i // tpb, i % tpb, 0)))
    return pl.pallas_call(
        functools.partial(_in_proj_kernel, tiles_per_seq=tpb),
        grid=(batch * tpb,),
        in_specs=[tile(d), _resident((1, d)), _resident((d, in_width)), _resident((1, 2 * d)),
                  _resident((CONV_KERNEL, CONV_WIDTH)), _resident((1, CONV_WIDTH))],
        out_specs=[qkv_specs[0], tile(2 * d), tile(CONV_WIDTH)] + qkv_specs[1:],
        out_shape=[qkv_shapes[0], jax.ShapeDtypeStruct((batch * seq, 2 * d), BF16),
                   jax.ShapeDtypeStruct((batch * seq, CONV_WIDTH), F32)] + qkv_shapes[1:],
        scratch_shapes=[pltpu.VMEM((d // LANES, tm, LANES), F32), pltpu.VMEM((tm, d), BF16),
                        pltpu.VMEM((tm, d), BF16), pltpu.VMEM((tm, d), BF16),
                        pltpu.VMEM((CONV_WIDTH // LANES, CONV_HALO + tm, LANES), F32)],
        compiler_params=pltpu.CompilerParams(dimension_semantics=("arbitrary",),
                                             vmem_limit_bytes=VMEM_LIMIT),
        name="in_proj",
    )(x.reshape(batch * seq, d), norm_g, w_in, gate_b, conv_w, conv_b)


def _attn_kernel(slope_ref, q0_ref, k0_ref, v0_ref, qkv1_ref, qkv2_ref, y_ref,
                 bias_ref, acc_ref, m_ref, l_ref):
    pair = pl.program_id(1)
    dilated = (None, qkv1_ref, qkv2_ref)

    def load(g, which, c, rows):
        if g == 0:
            return (q0_ref, k0_ref, v0_ref)[which][rows, :]
        return dilated[g][which, c, rows, :]

    qb = Q_BLOCK
    lane = lax.broadcasted_iota(jnp.int32, (qb, LANES), 1)
    first_head = lane < HEAD_DIM
    scale = HEAD_DIM ** -0.5
    head_scale = [jnp.where(first_head, scale, 0.0).astype(BF16),
                  jnp.where(first_head, 0.0, scale).astype(BF16)]

    qi = lax.broadcasted_iota(jnp.int32, (qb, 2 * qb), 0)
    kj = lax.broadcasted_iota(jnp.int32, (qb, 2 * qb), 1)
    rel = qb + qi - kj
    valid = (rel >= 0) & (rel <= qb)
    rel_f = rel.astype(F32)
    for g, (_, r) in enumerate(GROUPS):
        for h in range(HEADS_PER_STEP):
            slope = slope_ref[g * HEADS_PER_GROUP + pair * HEADS_PER_STEP + h]
            bias_ref[g, h * qb:(h + 1) * qb, :] = jnp.where(valid, -(slope * r) * rel_f, MASK_BIAS)

    def blocks(g, entries, with_prev):
        r = GROUPS[g][1]
        pitch = _row_pitch(r)
        bias = bias_ref[g] if with_prev else bias_ref[g, :, qb:]
        scores, values = [], []
        for c, n in entries:
            row = n * qb
            keys = pl.ds(row - qb, 2 * qb) if with_prev else pl.ds(row, qb)
            q = load(g, 0, c, pl.ds(row, qb))
            k = load(g, 1, c, keys)
            v = load(g, 2, c, keys)
            q2 = jnp.concatenate([q * head_scale[0], q * head_scale[1]], axis=0)
            scores.append(lax.dot_general(q2, k, (((1,), (1,)), ((), ())),
                                          preferred_element_type=F32) + bias)
            values.append(jnp.concatenate([v, jnp.ones_like(v)], axis=1))
        s = jnp.concatenate(scores, axis=0)
        m = jnp.max(s, axis=-1, keepdims=True)
        p = jnp.exp(s - m).astype(BF16)
        for i, (c, n) in enumerate(entries):
            lo = i * 2 * qb
            pv = jnp.dot(p[lo:lo + 2 * qb], values[i], preferred_element_type=F32)
            tok = n * (qb * pitch) + c
            rows = pl.ds(tok, qb, stride=pitch) if r > 1 else pl.ds(tok, qb)
            acc_ref[g, rows, :] = jnp.where(first_head, pv[:qb, :LANES], pv[qb:, :LANES])
            l_ref[g, rows, :] = jnp.where(first_head, pv[:qb, LANES:], pv[qb:, LANES:])
            m_ref[g, rows, :] = jnp.where(first_head, m[lo:lo + qb], m[lo + qb:lo + 2 * qb])

    for g, (_, r) in enumerate(GROUPS):
        first = [(c, 0) for c in range(r)]
        later = [(c, n) for c in range(r) for n in range(1, SEQ // r // qb)]
        for i in range(0, len(first), FIRST_BLOCK_BATCH):
            blocks(g, first[i:i + FIRST_BLOCK_BATCH], False)
        for entry in later:
            blocks(g, [entry], True)

    def mix(i, carry):
        chunk = 2 * qb
        rows = pl.ds(pl.multiple_of(i * chunk, chunk), chunk)

        def tokens(ref, g):
            r = GROUPS[g][1]
            pitch = _row_pitch(r)
            if pitch == r:
                return ref[g, rows, :]
            first = i * (chunk // r)
            return jnp.concatenate([ref[g, pl.ds((first + j) * pitch, r), :]
                                    for j in range(chunk // r)], axis=0)

        ms = [tokens(m_ref, g) for g in range(N_GROUPS)]
        top = jnp.maximum(jnp.maximum(ms[0], ms[1]), ms[2])
        num = jnp.zeros_like(top)
        den = jnp.zeros_like(top)
        for g in range(N_GROUPS):
            w = jnp.exp(ms[g] - top)
            num = num + w * tokens(acc_ref, g)
            den = den + w * tokens(l_ref, g)
        y_ref[0, rows, :] = (num / den).astype(BF16)
        return carry
    lax.fori_loop(0, SEQ // (2 * qb), mix, 0)


def _attention(qkvs, slopes):
    batch = qkvs[0].shape[0] // SEQ
    pairs = GROUP_WIDTH // LANES
    in_specs = [pl.BlockSpec(memory_space=pltpu.SMEM)]
    for which in range(3):
        in_specs.append(pl.BlockSpec((SEQ, LANES), lambda b, j, which=which: (b, which * pairs + j)))
    for _, r in GROUPS[1:]:
        in_specs.append(pl.BlockSpec((3, r, SEQ // r, LANES), lambda b, j: (0, b, 0, j)))
    result_rows = max((SEQ // r) * _row_pitch(r) for _, r in GROUPS)
    result_buf = pltpu.VMEM((N_GROUPS, result_rows, LANES), F32)
    return pl.pallas_call(
        _attn_kernel,
        grid=(batch, pairs),
        in_specs=in_specs,
        out_specs=pl.BlockSpec((1, SEQ, LANES), lambda b, j: (b, 0, j)),
        out_shape=jax.ShapeDtypeStruct((batch, SEQ, GROUP_WIDTH), BF16),
        scratch_shapes=[pltpu.VMEM((N_GROUPS, HEADS_PER_STEP * Q_BLOCK, 2 * Q_BLOCK), F32),
                        result_buf, result_buf, result_buf],
        compiler_params=pltpu.CompilerParams(dimension_semantics=("arbitrary", "arbitrary"),
                                             vmem_limit_bytes=VMEM_LIMIT),
        name="attn",
    )(slopes, qkvs[0], qkvs[0], qkvs[0], *qkvs[1:])


def _merge_ffn_kernel(x_ref, conv_ref, gate_ref, ya_ref, lg_ref, lb_ref, wc_ref, wa_ref, wo_ref,
                      g2_ref, wg_ref, wu_ref, wd_ref, gf_ref, o_ref, act_ref, h_ref, *, final_norm):
    tm = x_ref.shape[1]
    for row in range(0, tm, NORM_ROWS):
        c = conv_ref[0, row:row + NORM_ROWS, :]
        mu = jnp.mean(c, axis=-1, keepdims=True)
        d = c - mu
        var = jnp.mean(d * d, axis=-1, keepdims=True)
        y = d * lax.rsqrt(var + LN_EPS) * lg_ref[...] + lb_ref[...]
        act_ref[row:row + NORM_ROWS, :] = (y * _sigmoid(y)).astype(BF16)

    y_attn = jnp.dot(ya_ref[0], wa_ref[...], preferred_element_type=F32)
    y_conv = jnp.dot(act_ref[...], wc_ref[...], preferred_element_type=F32)
    g_attn = gate_ref[0, :, :D_MODEL].astype(F32)
    g_conv = gate_ref[0, :, D_MODEL:].astype(F32)
    merged = (g_attn * y_attn + g_conv * y_conv).astype(BF16)
    x = x_ref[0] + jnp.dot(merged, wo_ref[...], preferred_element_type=F32)

    h = x * lax.rsqrt(jnp.mean(x * x, axis=-1, keepdims=True) + RMS_EPS) * g2_ref[...]
    h_ref[...] = h.astype(BF16)
    out = x
    for lo in range(0, D_FF, FF_CHUNK):
        hi = min(lo + FF_CHUNK, D_FF)
        gate = jnp.dot(h_ref[...], wg_ref[:, lo:hi], preferred_element_type=F32)
        up = jnp.dot(h_ref[...], wu_ref[:, lo:hi], preferred_element_type=F32)
        act = (gate * _sigmoid(gate) * up).astype(BF16)
        out = out + jnp.dot(act, wd_ref[lo:hi, :], preferred_element_type=F32)
    if final_norm:
        out = out * lax.rsqrt(jnp.mean(out * out, axis=-1, keepdims=True) + RMS_EPS) * gf_ref[...]
    o_ref[0] = out


def _merge_ffn(x, conv, gates, y_attn, ln_g, ln_b, w_conv_out, w_attn_out, w_o,
               norm_g, w_gate, w_up, w_down, norm_f_g, final_norm):
    batch, seq, d = x.shape
    tm = TOKEN_TILE
    tile = lambda width: pl.BlockSpec((1, tm, width), lambda b, i: (b, i, 0))
    return pl.pallas_call(
        functools.partial(_merge_ffn_kernel, final_norm=final_norm),
        grid=(batch, seq // tm),
        in_specs=[tile(d), tile(CONV_WIDTH), tile(2 * d), tile(GROUP_WIDTH),
                  _resident((1, CONV_WIDTH)), _resident((1, CONV_WIDTH)),
                  _resident((CONV_WIDTH, d)), _resident((GROUP_WIDTH, d)), _resident((d, d)),
                  _resident((1, d)), _resident((d, D_FF)), _resident((d, D_FF)),
                  _resident((D_FF, d)), _resident((1, d))],
        out_specs=tile(d),
        out_shape=jax.ShapeDtypeStruct((batch, seq, d), F32),
        scratch_shapes=[pltpu.VMEM((tm, CONV_WIDTH), BF16), pltpu.VMEM((tm, d), BF16)],
        compiler_params=pltpu.CompilerParams(dimension_semantics=("arbitrary", "arbitrary"),
                                             vmem_limit_bytes=VMEM_LIMIT),
        name="merge_ffn",
    )(x, conv, gates, y_attn, ln_g, ln_b, w_conv_out, w_attn_out, w_o,
      norm_g, w_gate, w_up, w_down, norm_f_g)


def kernel(x, norm1_g, w_in, gate_b, conv_w, conv_b, conv_ln_g, conv_ln_b, w_conv_out, w_attn_out,
           w_o, norm2_g, w_ffn_gate, w_ffn_up, w_ffn_down, norm_f_g):
    depth = w_in.shape[0]
    slopes = jnp.asarray(_alibi_slope_table())
    row = lambda a: a.reshape(1, -1)
    for l in range(depth):
        qkv0, gates, conv, *dilated = _in_proj(x, row(norm1_g[l]), w_in[l].astype(BF16),
                                               row(gate_b[l]), conv_w[l], row(conv_b[l]))
        y_attn = _attention([qkv0] + dilated, slopes)
        conv = conv.reshape(x.shape[0], x.shape[1], -1)
        gates = gates.reshape(x.shape[0], x.shape[1], -1)
        x = _merge_ffn(x, conv, gates, y_attn, row(conv_ln_g[l]), row(conv_ln_b[l]),
                       w_conv_out[l].astype(BF16), w_attn_out[l].astype(BF16), w_o[l].astype(BF16),
                       row(norm2_g[l]), w_ffn_gate[l].astype(BF16), w_ffn_up[l].astype(BF16),
                       w_ffn_down[l].astype(BF16), row(norm_f_g), final_norm=(l == depth - 1))
    return x
```

```python
import functools
import math

import numpy as np
import jax
import jax.numpy as jnp
from jax import lax
from jax.experimental import pallas as pl
from jax.experimental.pallas import tpu as pltpu

D_MODEL = 1024
SEQ = 2048
HEAD_DIM = 64
GROUPS = ((128, 1), (512, 4), (2048, 16))
N_GROUPS = len(GROUPS)
HEADS_PER_GROUP = 8
GROUP_WIDTH = HEADS_PER_GROUP * HEAD_DIM
ATTN_WIDTH = N_GROUPS * GROUP_WIDTH
Q_BLOCK = 128
CONV_WIDTH = D_MODEL
CONV_KERNEL = 31
D_FF = 2816
RMS_EPS = 1e-6
LN_EPS = 1e-5
MASK_BIAS = -1e30

LANES = 128
SUBLANES = 8
HEADS_PER_STEP = LANES // HEAD_DIM
TOKEN_TILE = 512
CONV_HALO = 32
CONV_ROWS = 16
NORM_ROWS = 16
FIRST_BLOCK_BATCH = 2
FF_CHUNK = 512
VMEM_LIMIT = 56 * 1024 * 1024

U_OFF = 3 * ATTN_WIDTH
GATE_OFF = U_OFF + 2 * CONV_WIDTH

BF16 = jnp.bfloat16
F32 = jnp.float32


def _alibi_slope_table():
    def pow2(m):
        start = 2.0 ** (-8.0 / m)
        return [start ** (i + 1) for i in range(m)]

    def slopes(n):
        if math.log2(n).is_integer():
            return pow2(n)
        c = 2 ** math.floor(math.log2(n))
        return pow2(c) + slopes(2 * c)[0::2][: n - c]

    s = sorted(slopes(N_GROUPS * HEADS_PER_GROUP), reverse=True)
    return np.asarray(s, dtype=np.float32)


def _resident(shape):
    return pl.BlockSpec(shape, lambda *_: (0,) * len(shape), pipeline_mode=pl.Buffered(1))


def _sigmoid(z):
    return 0.5 * jnp.tanh(0.5 * z) + 0.5


def _row_pitch(dilation):
    return dilation + 1 if dilation % SUBLANES == 0 else dilation


def _in_proj_kernel(x_ref, g_ref, w_ref, gb_ref, cw_ref, cb_ref,
                    qkv0_ref, gate_ref, conv_ref, qkv1_ref, qkv2_ref,
                    hn_ref, h1_ref, h4_ref, h16_ref, cbuf_ref, *, tiles_per_seq):
    tm = x_ref.shape[0]
    n_slabs = cbuf_ref.shape[0]

    seq_start = pl.program_id(0) % tiles_per_seq == 0

    @pl.when(seq_start)
    def _():
        cbuf_ref[:, 0:CONV_HALO, :] = jnp.zeros((n_slabs, CONV_HALO, LANES), F32)

    @pl.when(jnp.logical_not(seq_start))
    def _():
        cbuf_ref[:, 0:CONV_HALO, :] = cbuf_ref[:, tm:tm + CONV_HALO, :]

    x = x_ref[...]
    hn = x * lax.rsqrt(jnp.mean(x * x, axis=-1, keepdims=True) + RMS_EPS) * g_ref[...]
    h1_ref[...] = hn.astype(BF16)
    for cb in range(hn_ref.shape[0]):
        hn_ref[cb] = hn[:, cb * LANES:(cb + 1) * LANES]
    for r, h_ref in ((GROUPS[1][1], h4_ref), (GROUPS[2][1], h16_ref)):
        rows = tm // r
        for c in range(r):
            for cb in range(hn_ref.shape[0]):
                h_ref[c * rows:(c + 1) * rows, cb * LANES:(cb + 1) * LANES] = (
                    hn_ref[cb, pl.ds(c, rows, stride=r), :].astype(BF16))

    h1 = h1_ref[...]
    slabs_per_dot = GROUP_WIDTH // LANES
    for jj in range(CONV_WIDTH // GROUP_WIDTH):
        lo = jj * GROUP_WIDTH
        ua = jnp.dot(h1, w_ref[:, U_OFF + lo:U_OFF + lo + GROUP_WIDTH], preferred_element_type=F32)
        ub = jnp.dot(h1, w_ref[:, U_OFF + CONV_WIDTH + lo:U_OFF + CONV_WIDTH + lo + GROUP_WIDTH],
                     preferred_element_type=F32)
        glu = ua * _sigmoid(ub)
        for s in range(slabs_per_dot):
            cbuf_ref[jj * slabs_per_dot + s, CONV_HALO:, :] = glu[:, s * LANES:(s + 1) * LANES]

    for g, (h_ref, out_ref) in enumerate(((h1_ref, qkv0_ref), (h4_ref, qkv1_ref), (h16_ref, qkv2_ref))):
        r = GROUPS[g][1]
        rows = tm // r
        for which in range(3):
            col = which * ATTN_WIDTH + g * GROUP_WIDTH
            acc = jnp.dot(h_ref[...], w_ref[:, col:col + GROUP_WIDTH],
                          preferred_element_type=F32).astype(BF16)
            if r == 1:
                out_ref[:, which * GROUP_WIDTH:(which + 1) * GROUP_WIDTH] = acc
            else:
                for c in range(r):
                    out_ref[which, c] = acc[c * rows:(c + 1) * rows]

    for jj in range(2 * D_MODEL // GROUP_WIDTH):
        lo = jj * GROUP_WIDTH
        gl = jnp.dot(h1, w_ref[:, GATE_OFF + lo:GATE_OFF + lo + GROUP_WIDTH], preferred_element_type=F32)
        gate_ref[:, lo:lo + GROUP_WIDTH] = _sigmoid(gl + gb_ref[:, lo:lo + GROUP_WIDTH]).astype(BF16)

    for row in range(0, tm, CONV_ROWS):
        for cb in range(n_slabs):
            cols = slice(cb * LANES, (cb + 1) * LANES)
            acc = jnp.broadcast_to(cb_ref[:, cols], (CONV_ROWS, LANES))
            for k in range(CONV_KERNEL):
                src = row + CONV_HALO - CONV_KERNEL + 1 + k
                acc = acc + cbuf_ref[cb, src:src + CONV_ROWS, :] * cw_ref[k:k + 1, cols]
            conv_ref[row:row + CONV_ROWS, cols] = acc


def _in_proj(x, norm_g, w_in, gate_b, conv_w, conv_b):
    batch, seq, d = x.shape
    tm = TOKEN_TILE
    tpb = seq // tm
    in_width = w_in.shape[1]
    tile = lambda width: pl.BlockSpec((tm, width), lambda i: (i, 0))
    qkv_shapes = [jax.ShapeDtypeStruct((batch * seq, 3 * GROUP_WIDTH), BF16)]
    qkv_specs = [tile(3 * GROUP_WIDTH)]
    for _, r in GROUPS[1:]:
        qkv_shapes.append(jax.ShapeDtypeStruct((3, batch * r, seq // r, GROUP_WIDTH), BF16))
        qkv_specs.append(pl.BlockSpec((3, r, tm // r, GROUP_WIDTH),
                                      lambda i: (0, i // tpb, i % tpb, 0)))
    return pl.pallas_call(
        functools.partial(_in_proj_kernel, tiles_per_seq=tpb),
        grid=(batch * tpb,),
        in_specs=[tile(d), _resident((1, d)), _resident((d, in_width)), _resident((1, 2 * d)),
                  _resident((CONV_KERNEL, CONV_WIDTH)), _resident((1, CONV_WIDTH))],
        out_specs=[qkv_specs[0], tile(2 * d), tile(CONV_WIDTH)] + qkv_specs[1:],
        out_shape=[qkv_shapes[0], jax.ShapeDtypeStruct((batch * seq, 2 * d), BF16),
                   jax.ShapeDtypeStruct((batch * seq, CONV_WIDTH), F32)] + qkv_shapes[1:],
        scratch_shapes=[pltpu.VMEM((d // LANES, tm, LANES), F32), pltpu.VMEM((tm, d), BF16),
                        pltpu.VMEM((tm, d), BF16), pltpu.VMEM((tm, d), BF16),
                        pltpu.VMEM((CONV_WIDTH // LANES, CONV_HALO + tm, LANES), F32)],
        compiler_params=pltpu.CompilerParams(dimension_semantics=("arbitrary",),
                                             vmem_limit_bytes=VMEM_LIMIT),
        name="in_proj",
    )(x.reshape(batch * seq, d), norm_g, w_in, gate_b, conv_w, conv_b)


def _attn_kernel(slope_ref, q0_ref, k0_ref, v0_ref, qkv1_ref, qkv2_ref, y_ref,
                 bias_ref, acc_ref, m_ref, l_ref):
    pair = pl.program_id(1)
    dilated = (None, qkv1_ref, qkv2_ref)

    def load(g, which, c, rows):
        if g == 0:
            return (q0_ref, k0_ref, v0_ref)[which][rows, :]
        return dilated[g][which, c, rows, :]

    qb = Q_BLOCK
    lane = lax.broadcasted_iota(jnp.int32, (qb, LANES), 1)
    first_head = lane < HEAD_DIM
    scale = HEAD_DIM ** -0.5
    head_scale = [jnp.where(first_head, scale, 0.0).astype(BF16),
                  jnp.where(first_head, 0.0, scale).astype(BF16)]

    qi = lax.broadcasted_iota(jnp.int32, (qb, 2 * qb), 0)
    kj = lax.broadcasted_iota(jnp.int32, (qb, 2 * qb), 1)
    rel = qb + qi - kj
    valid = (rel >= 0) & (rel <= qb)
    rel_f = rel.astype(F32)
    for g, (_, r) in enumerate(GROUPS):
        for h in range(HEADS_PER_STEP):
            slope = slope_ref[g * HEADS_PER_GROUP + pair * HEADS_PER_STEP + h]
            bias_ref[g, h * qb:(h + 1) * qb, :] = jnp.where(valid, -(slope * r) * rel_f, MASK_BIAS)

    def blocks(g, entries, with_prev):
        r = GROUPS[g][1]
        pitch = _row_pitch(r)
        bias = bias_ref[g] if with_prev else bias_ref[g, :, qb:]
        scores, values = [], []
        for c, n in entries:
            row = n * qb
            keys = pl.ds(row - qb, 2 * qb) if with_prev else pl.ds(row, qb)
            q = load(g, 0, c, pl.ds(row, qb))
            k = load(g, 1, c, keys)
            v = load(g, 2, c, keys)
            q2 = jnp.concatenate([q * head_scale[0], q * head_scale[1]], axis=0)
            scores.append(lax.dot_general(q2, k, (((1,), (1,)), ((), ())),
                                          preferred_element_type=F32) + bias)
            values.append(jnp.concatenate([v, jnp.ones_like(v)], axis=1))
        s = jnp.concatenate(scores, axis=0)
        m = jnp.max(s, axis=-1, keepdims=True)
        p = jnp.exp(s - m).astype(BF16)
        for i, (c, n) in enumerate(entries):
            lo = i * 2 * qb
            pv = jnp.dot(p[lo:lo + 2 * qb], values[i], preferred_element_type=F32)
            tok = n * (qb * pitch) + c
            rows = pl.ds(tok, qb, stride=pitch) if r > 1 else pl.ds(tok, qb)
            acc_ref[g, rows, :] = jnp.where(first_head, pv[:qb, :LANES], pv[qb:, :LANES])
            l_ref[g, rows, :] = jnp.where(first_head, pv[:qb, LANES:], pv[qb:, LANES:])
            m_ref[g, rows, :] = jnp.where(first_head, m[lo:lo + qb], m[lo + qb:lo + 2 * qb])

    for g, (_, r) in enumerate(GROUPS):
        first = [(c, 0) for c in range(r)]
        later = [(c, n) for c in range(r) for n in range(1, SEQ // r // qb)]
        for i in range(0, len(first), FIRST_BLOCK_BATCH):
            blocks(g, first[i:i + FIRST_BLOCK_BATCH], False)
        for entry in later:
            blocks(g, [entry], True)

    def mix(i, carry):
        chunk = 2 * qb
        rows = pl.ds(i * chunk, chunk)

        def tokens(ref, g):
            r = GROUPS[g][1]
            pitch = _row_pitch(r)
            if pitch == r:
                return ref[g, rows, :]
            first = i * (chunk // r)
            return jnp.concatenate([ref[g, pl.ds((first + j) * pitch, r), :]
                                    for j in range(chunk // r)], axis=0)

        ms = [tokens(m_ref, g) for g in range(N_GROUPS)]
        top = jnp.maximum(jnp.maximum(ms[0], ms[1]), ms[2])
        num = jnp.zeros_like(top)
        den = jnp.zeros_like(top)
        for g in range(N_GROUPS):
            w = jnp.exp(ms[g] - top)
            num = num + w * tokens(acc_ref, g)
            den = den + w * tokens(l_ref, g)
        y_ref[0, rows, :] = (num / den).astype(BF16)
        return carry
    for i in range(SEQ // (2 * qb)):
        mix(i, 0)


def _attention(qkvs, slopes):
    batch = qkvs[0].shape[0] // SEQ
    pairs = GROUP_WIDTH // LANES
    in_specs = [pl.BlockSpec(memory_space=pltpu.SMEM)]
    for which in range(3):
        in_specs.append(pl.BlockSpec((SEQ, LANES), lambda b, j, which=which: (b, which * pairs + j)))
    for _, r in GROUPS[1:]:
        in_specs.append(pl.BlockSpec((3, r, SEQ // r, LANES), lambda b, j: (0, b, 0, j)))
    result_rows = max((SEQ // r) * _row_pitch(r) for _, r in GROUPS)
    result_buf = pltpu.VMEM((N_GROUPS, result_rows, LANES), F32)
    return pl.pallas_call(
        _attn_kernel,
        grid=(batch, pairs),
        in_specs=in_specs,
        out_specs=pl.BlockSpec((1, SEQ, LANES), lambda b, j: (b, 0, j)),
        out_shape=jax.ShapeDtypeStruct((batch, SEQ, GROUP_WIDTH), BF16),
        scratch_shapes=[pltpu.VMEM((N_GROUPS, HEADS_PER_STEP * Q_BLOCK, 2 * Q_BLOCK), F32),
                        result_buf, result_buf, result_buf],
        compiler_params=pltpu.CompilerParams(dimension_semantics=("arbitrary", "arbitrary"),
                                             vmem_limit_bytes=VMEM_LIMIT),
        name="attn",
    )(slopes, qkvs[0], qkvs[0], qkvs[0], *qkvs[1:])


def _merge_ffn_kernel(x_ref, conv_ref, gate_ref, ya_ref, lg_ref, lb_ref, wc_ref, wa_ref, wo_ref,
                      g2_ref, wg_ref, wu_ref, wd_ref, gf_ref, o_ref, act_ref, h_ref, *, final_norm):
    tm = x_ref.shape[1]
    for row in range(0, tm, NORM_ROWS):
        c = conv_ref[0, row:row + NORM_ROWS, :]
        mu = jnp.mean(c, axis=-1, keepdims=True)
        d = c - mu
        var = jnp.mean(d * d, axis=-1, keepdims=True)
        y = d * lax.rsqrt(var + LN_EPS) * lg_ref[...] + lb_ref[...]
        act_ref[row:row + NORM_ROWS, :] = (y * _sigmoid(y)).astype(BF16)

    y_attn = jnp.dot(ya_ref[0], wa_ref[...], preferred_element_type=F32)
    y_conv = jnp.dot(act_ref[...], wc_ref[...], preferred_element_type=F32)
    g_attn = gate_ref[0, :, :D_MODEL].astype(F32)
    g_conv = gate_ref[0, :, D_MODEL:].astype(F32)
    merged = (g_attn * y_attn + g_conv * y_conv).astype(BF16)
    x = x_ref[0] + jnp.dot(merged, wo_ref[...], preferred_element_type=F32)

    h = x * lax.rsqrt(jnp.mean(x * x, axis=-1, keepdims=True) + RMS_EPS) * g2_ref[...]
    h_ref[...] = h.astype(BF16)
    out = x
    for lo in range(0, D_FF, FF_CHUNK):
        hi = min(lo + FF_CHUNK, D_FF)
        gate = jnp.dot(h_ref[...], wg_ref[:, lo:hi], preferred_element_type=F32)
        up = jnp.dot(h_ref[...], wu_ref[:, lo:hi], preferred_element_type=F32)
        act = (gate * _sigmoid(gate) * up).astype(BF16)
        out = out + jnp.dot(act, wd_ref[lo:hi, :], preferred_element_type=F32)
    if final_norm:
        out = out * lax.rsqrt(jnp.mean(out * out, axis=-1, keepdims=True) + RMS_EPS) * gf_ref[...]
    o_ref[0] = out


def _merge_ffn(x, conv, gates, y_attn, ln_g, ln_b, w_conv_out, w_attn_out, w_o,
               norm_g, w_gate, w_up, w_down, norm_f_g, final_norm):
    batch, seq, d = x.shape
    tm = TOKEN_TILE
    tile = lambda width: pl.BlockSpec((1, tm, width), lambda b, i: (b, i, 0))
    return pl.pallas_call(
        functools.partial(_merge_ffn_kernel, final_norm=final_norm),
        grid=(batch, seq // tm),
        in_specs=[tile(d), tile(CONV_WIDTH), tile(2 * d), tile(GROUP_WIDTH),
                  _resident((1, CONV_WIDTH)), _resident((1, CONV_WIDTH)),
                  _resident((CONV_WIDTH, d)), _resident((GROUP_WIDTH, d)), _resident((d, d)),
                  _resident((1, d)), _resident((d, D_FF)), _resident((d, D_FF)),
                  _resident((D_FF, d)), _resident((1, d))],
        out_specs=tile(d),
        out_shape=jax.ShapeDtypeStruct((batch, seq, d), F32),
        scratch_shapes=[pltpu.VMEM((tm, CONV_WIDTH), BF16), pltpu.VMEM((tm, d), BF16)],
        compiler_params=pltpu.CompilerParams(dimension_semantics=("arbitrary", "arbitrary"),
                                             vmem_limit_bytes=VMEM_LIMIT),
        name="merge_ffn",
    )(x, conv, gates, y_attn, ln_g, ln_b, w_conv_out, w_attn_out, w_o,
      norm_g, w_gate, w_up, w_down, norm_f_g)


def kernel(x, norm1_g, w_in, gate_b, conv_w, conv_b, conv_ln_g, conv_ln_b, w_conv_out, w_attn_out,
           w_o, norm2_g, w_ffn_gate, w_ffn_up, w_ffn_down, norm_f_g):
    depth = w_in.shape[0]
    slopes = jnp.asarray(_alibi_slope_table())
    row = lambda a: a.reshape(1, -1)
    for l in range(depth):
        qkv0, gates, conv, *dilated = _in_proj(x, row(norm1_g[l]), w_in[l].astype(BF16),
                                               row(gate_b[l]), conv_w[l], row(conv_b[l]))
        y_attn = _attention([qkv0] + dilated, slopes)
        conv = conv.reshape(x.shape[0], x.shape[1], -1)
        gates = gates.reshape(x.shape[0], x.shape[1], -1)
        x = _merge_ffn(x, conv, gates, y_attn, row(conv_ln_g[l]), row(conv_ln_b[l]),
                       w_conv_out[l].astype(BF16), w_attn_out[l].astype(BF16), w_o[l].astype(BF16),
                       row(norm2_g[l]), w_ffn_gate[l].astype(BF16), w_ffn_up[l].astype(BF16),
                       w_ffn_down[l].astype(BF16), row(norm_f_g), final_norm=(l == depth - 1))
    return x
```
